```python
import jax, jax.numpy as jnp
from jax import lax
import numpy as np

D_MODEL = 2048
BATCH = 4
SEQ = 2048
DEPTH = 4
DEC_BATCH = 128
DEC_SEQ = 1
PAST_LEN = 8192
PAGE_SIZE = 128

V_HEAD_DIM = 128
MLA_WIDTH = D_MODEL // 2
MLA_HEADS = MLA_WIDTH // V_HEAD_DIM
QK_NOPE_DIM = 128
QK_ROPE_DIM = 64
Q_LORA_RANK = 512
KV_LORA_RANK = 512
ROPE_THETA = 10000.0
MLA_SCALE = (QK_NOPE_DIM + QK_ROPE_DIM) ** -0.5
Q_BLOCK = 128
CHUNK = 128
CHUNK_CH = 128
CHUNK_WIDTH = D_MODEL - MLA_WIDTH
CHUNK_GROUPS = CHUNK_WIDTH // CHUNK_CH
MIX_WIDTH = MLA_WIDTH + CHUNK_WIDTH
IN_WIDTH = Q_LORA_RANK + KV_LORA_RANK + QK_ROPE_DIM + 2 * CHUNK_WIDTH
MEM_TOKENS = 256
MEM_HEADS = 4
MEM_HEAD_DIM = 128
MEM_SCALE = MEM_HEAD_DIM ** -0.5
FFN_DIM = 5632
N_EXPERTS = 8
TOP_K = 2
EXPERT_FF = 2816
RMS_EPS = 1e-6

kernel_name = "mla_chunkmlp_hybrid_decoder_step"


def rmsnorm(x, g):
    xf = x.astype(jnp.float32)
    y = xf * lax.rsqrt(jnp.mean(xf * xf, axis=-1, keepdims=True) + RMS_EPS)
    return (y * g.astype(jnp.float32)).astype(x.dtype)


def rope(x, pos):
    half = QK_ROPE_DIM // 2
    inv_freq = 1.0 / (ROPE_THETA ** (jnp.arange(half, dtype=jnp.float32) / half))
    ang = pos.astype(jnp.float32)[:, None] * inv_freq[None, :]
    cos, sin = jnp.cos(ang), jnp.sin(ang)
    if x.ndim == 4:
        cos, sin = cos[:, None, :], sin[:, None, :]
    xf = x.astype(jnp.float32)
    x1, x2 = xf[..., :half], xf[..., half:]
    return jnp.concatenate([x1 * cos - x2 * sin, x1 * sin + x2 * cos], axis=-1).astype(x.dtype)


def mixer_projections(h, pos, w_in, q_norm, w_uq, kv_norm, w_uk, chunk_v_norm):
    b, l, _ = h.shape
    z = h @ w_in
    o1 = Q_LORA_RANK
    o2 = o1 + KV_LORA_RANK
    o3 = o2 + QK_ROPE_DIM
    o4 = o3 + CHUNK_WIDTH
    c_q, c_kv, k_r, u, v = z[..., :o1], z[..., o1:o2], z[..., o2:o3], z[..., o3:o4], z[..., o4:]
    q = jnp.einsum('blr,rhk->blhk', rmsnorm(c_q, q_norm), w_uq)
    q_abs = jnp.einsum('blhn,chn->blhc', q[..., :QK_NOPE_DIM], w_uk)
    q_rope = rope(q[..., QK_NOPE_DIM:], pos)
    c_kv = rmsnorm(c_kv, kv_norm)
    k_r = rope(k_r, pos)
    u = jax.nn.gelu(u).reshape(b, l, CHUNK_GROUPS, CHUNK_CH)
    v = rmsnorm(jax.nn.gelu(v).reshape(b, l, CHUNK_GROUPS, CHUNK_CH), chunk_v_norm)
    return q_abs, q_rope, c_kv, k_r, u, v


def mla_prompt(q_abs, q_rope, c_kv, k_r):
    b, l, h, c = q_abs.shape
    nb = l // Q_BLOCK
    qa = q_abs.reshape(b, nb, Q_BLOCK, h, c).swapaxes(0, 1)
    qr = q_rope.reshape(b, nb, Q_BLOCK, h, QK_ROPE_DIM).swapaxes(0, 1)
    k_pos = jnp.arange(l)

    def block(args):
        qa_b, qr_b, start = args
        s = (jnp.einsum('bqhc,bkc->bhqk', qa_b, c_kv)
             + jnp.einsum('bqhr,bkr->bhqk', qr_b, k_r)).astype(jnp.float32) * MLA_SCALE
        q_pos = start + jnp.arange(Q_BLOCK)
        s = jnp.where(k_pos[None, :] <= q_pos[:, None], s, -jnp.inf)
        p = jax.nn.softmax(s, axis=-1).astype(c_kv.dtype)
        return jnp.einsum('bhqk,bkc->bqhc', p, c_kv)

    o = lax.map(block, (qa, qr, jnp.arange(nb) * Q_BLOCK))
    return o.swapaxes(0, 1).reshape(b, l, h, c)


def mla_sample(q_abs, q_rope, c_kv, k_r, lat_past, kr_past):
    l = c_kv.shape[1]
    past = lat_past.shape[1]
    s_past = (jnp.einsum('bqhc,bkc->bhqk', q_abs, lat_past)
              + jnp.einsum('bqhr,bkr->bhqk', q_rope, kr_past)).astype(jnp.float32)
    s_new = (jnp.einsum('bqhc,bkc->bhqk', q_abs, c_kv)
             + jnp.einsum('bqhr,bkr->bhqk', q_rope, k_r)).astype(jnp.float32)
    s_new = jnp.where(jnp.tril(jnp.ones((l, l), dtype=bool)), s_new, -jnp.inf)
    s = jnp.concatenate([s_past, s_new], axis=-1) * MLA_SCALE
    p = jax.nn.softmax(s, axis=-1).astype(c_kv.dtype)
    return (jnp.einsum('bhqk,bkc->bqhc', p[..., :past], lat_past)
            + jnp.einsum('bhqk,bkc->bqhc', p[..., past:], c_kv))


def chunk_mix(u, v, w_s, b_s):
    b, l, g, c = v.shape
    n = -(-l // CHUNK)
    vp = jnp.pad(v, ((0, 0), (0, n * CHUNK - l), (0, 0), (0, 0))).reshape(b, n, CHUNK, g, c)
    mixed = jnp.einsum('gts,bnsgc->bntgc', jnp.tril(w_s), vp) + b_s.T[:, :, None]
    return u * mixed.reshape(b, n * CHUNK, g, c)[:, :l]


def mixer_output(o_lat, u, v, w_uv, w_s, b_s, g_mla, g_chunk, w_out):
    b, l = o_lat.shape[:2]
    o_att = jnp.einsum('blhc,chv->blhv', o_lat, w_uv).reshape(b, l, MLA_WIDTH)
    o_chk = chunk_mix(u, v, w_s, b_s).reshape(b, l, CHUNK_WIDTH)
    cat = jnp.concatenate([rmsnorm(o_att, g_mla), rmsnorm(o_chk, g_chunk)], axis=-1)
    return cat @ w_out


def mem_kv(mem, g, w_k, w_v):
    m = rmsnorm(mem, g)
    return jnp.einsum('bmd,dhk->bmhk', m, w_k), jnp.einsum('bmd,dhk->bmhk', m, w_v)


def mem_attn(h, k, v, w_q, w_o):
    q = jnp.einsum('bld,dhk->blhk', h, w_q)
    s = jnp.einsum('blhk,bmhk->bhlm', q, k).astype(jnp.float32) * MEM_SCALE
    p = jax.nn.softmax(s, axis=-1).astype(v.dtype)
    o = jnp.einsum('bhlm,bmhk->blhk', p, v)
    return jnp.einsum('blhk,hkd->bld', o, w_o)


def swiglu(h, wg, wu, wd):
    return (jax.nn.silu(h @ wg) * (h @ wu)) @ wd


def moe(h, w_router, wg, wu, wd):
    logits = (h @ w_router).astype(jnp.float32)
    top_val, top_idx = lax.top_k(logits, TOP_K)
    w = jax.nn.softmax(top_val, axis=-1)
    gate = jnp.sum(jax.nn.one_hot(top_idx, N_EXPERTS, dtype=jnp.float32) * w[..., None], axis=-2).astype(h.dtype)
    y = jnp.zeros_like(h)
    for e in range(N_EXPERTS):
        y = y + gate[..., e:e + 1] * swiglu(h, wg[e], wu[e], wd[e])
    return y


def setup_inputs(seed: int = 0) -> dict:
    key = jax.random.key(seed)
    ks = iter(jax.random.split(key, 40))
    f32 = jnp.float32

    def nrm(shape, scale):
        return jax.random.normal(next(ks), shape, f32) * scale

    def gain(shape, s=0.05):
        return 1.0 + s * jax.random.normal(next(ks), shape, f32)

    n_pages = PAST_LEN // PAGE_SIZE
    n_pool = (DEC_BATCH * n_pages * 5) // 4
    n_dense = (DEPTH + 1) // 2
    n_moe = DEPTH // 2
    d = D_MODEL
    out = {}
    out['x_prompt'] = nrm((BATCH, SEQ, d), 1.0)
    out['x_sample'] = nrm((DEC_BATCH, DEC_SEQ, d), 1.0)
    out['mem_prompt'] = nrm((BATCH, MEM_TOKENS, d), 1.0)
    out['cache_kv_latent'] = nrm((DEPTH, n_pool, PAGE_SIZE, KV_LORA_RANK), 1.0)
    out['cache_k_rope'] = nrm((DEPTH, n_pool, PAGE_SIZE, QK_ROPE_DIM), 1.0)
    out['cache_mem_k'] = nrm((DEPTH, DEC_BATCH, MEM_TOKENS, MEM_HEADS, MEM_HEAD_DIM), 1.0)
    out['cache_mem_v'] = nrm((DEPTH, DEC_BATCH, MEM_TOKENS, MEM_HEADS, MEM_HEAD_DIM), 1.0)
    perm = jax.random.permutation(next(ks), n_pool)
    out['page_table'] = perm[:DEC_BATCH * n_pages].reshape(DEC_BATCH, n_pages).astype(jnp.int32)
    out['norm_mix'] = gain((DEPTH, d))
    out['w_in'] = nrm((DEPTH, d, IN_WIDTH), d ** -0.5)
    out['q_norm'] = gain((DEPTH, Q_LORA_RANK))
    out['w_uq'] = nrm((DEPTH, Q_LORA_RANK, MLA_HEADS, QK_NOPE_DIM + QK_ROPE_DIM), Q_LORA_RANK ** -0.5)
    out['kv_norm'] = gain((DEPTH, KV_LORA_RANK))
    out['w_uk'] = nrm((DEPTH, KV_LORA_RANK, MLA_HEADS, QK_NOPE_DIM), KV_LORA_RANK ** -0.5)
    out['w_uv'] = nrm((DEPTH, KV_LORA_RANK, MLA_HEADS, V_HEAD_DIM), KV_LORA_RANK ** -0.5)
    out['chunk_v_norm'] = gain((DEPTH, CHUNK_GROUPS, CHUNK_CH))
    out['w_spatial'] = nrm((DEPTH, CHUNK_GROUPS, CHUNK, CHUNK), CHUNK ** -0.5)
    out['b_spatial'] = gain((DEPTH, CHUNK_GROUPS, CHUNK), 0.1)
    out['out_norm_mla'] = gain((DEPTH, MLA_WIDTH))
    out['out_norm_chunk'] = gain((DEPTH, CHUNK_WIDTH))
    out['w_out'] = nrm((DEPTH, MIX_WIDTH, d), MIX_WIDTH ** -0.5)
    out['norm_mem_q'] = gain((DEPTH, d))
    out['norm_mem_kv'] = gain((DEPTH, d))
    out['w_mem_q'] = nrm((DEPTH, d, MEM_HEADS, MEM_HEAD_DIM), d ** -0.5)
    out['w_mem_k'] = nrm((DEPTH, d, MEM_HEADS, MEM_HEAD_DIM), d ** -0.5)
    out['w_mem_v'] = nrm((DEPTH, d, MEM_HEADS, MEM_HEAD_DIM), d ** -0.5)
    out['w_mem_o'] = nrm((DEPTH, MEM_HEADS, MEM_HEAD_DIM, d), (MEM_HEADS * MEM_HEAD_DIM) ** -0.5)
    out['norm_ffn'] = gain((DEPTH, d))
    out['w_gate_dense'] = nrm((n_dense, d, FFN_DIM), d ** -0.5)
    out['w_up_dense'] = nrm((n_dense, d, FFN_DIM), d ** -0.5)
    out['w_down_dense'] = nrm((n_dense, FFN_DIM, d), FFN_DIM ** -0.5)
    out['w_router'] = nrm((n_moe, d, N_EXPERTS), d ** -0.5)
    out['w_gate_moe'] = nrm((n_moe, N_EXPERTS, d, EXPERT_FF), d ** -0.5)
    out['w_up_moe'] = nrm((n_moe, N_EXPERTS, d, EXPERT_FF), d ** -0.5)
    out['w_down_moe'] = nrm((n_moe, N_EXPERTS, EXPERT_FF, d), EXPERT_FF ** -0.5)
    out['final_norm'] = gain((d,))
    return out


def reference(x_prompt, x_sample, mem_prompt, cache_kv_latent, cache_k_rope, cache_mem_k, cache_mem_v,
              page_table, norm_mix, w_in, q_norm, w_uq, kv_norm, w_uk, w_uv, chunk_v_norm, w_spatial,
              b_spatial, out_norm_mla, out_norm_chunk, w_out, norm_mem_q, norm_mem_kv, w_mem_q, w_mem_k,
              w_mem_v, w_mem_o, norm_ffn, w_gate_dense, w_up_dense, w_down_dense, w_router, w_gate_moe,
              w_up_moe, w_down_moe, final_norm):
    xp, xs = x_prompt, x_sample
    dec_b, dec_l = xs.shape[0], xs.shape[1]
    past_len = page_table.shape[1] * cache_kv_latent.shape[2]
    pos_p = jnp.arange(xp.shape[1])
    pos_s = past_len + jnp.arange(dec_l)
    lat_p_l, kr_p_l, mk_p_l, mv_p_l, lat_s_l, kr_s_l, v_s_l = [], [], [], [], [], [], []
    for l in range(DEPTH):
        proj = (w_in[l], q_norm[l], w_uq[l], kv_norm[l], w_uk[l], chunk_v_norm[l])
        qa_p, qr_p, ckv_p, kr_p, u_p, v_p = mixer_projections(rmsnorm(xp, norm_mix[l]), pos_p, *proj)
        qa_s, qr_s, ckv_s, kr_s, u_s, v_s = mixer_projections(rmsnorm(xs, norm_mix[l]), pos_s, *proj)
        olat_p = mla_prompt(qa_p, qr_p, ckv_p, kr_p)
        lat_past = cache_kv_latent[l, page_table].reshape(dec_b, past_len, KV_LORA_RANK)
        kr_past = cache_k_rope[l, page_table].reshape(dec_b, past_len, QK_ROPE_DIM)
        olat_s = mla_sample(qa_s, qr_s, ckv_s, kr_s, lat_past, kr_past)
        outp = (w_uv[l], w_spatial[l], b_spatial[l], out_norm_mla[l], out_norm_chunk[l], w_out[l])
        xp = xp + mixer_output(olat_p, u_p, v_p, *outp)
        xs = xs + mixer_output(olat_s, u_s, v_s, *outp)
        lat_p_l.append(ckv_p)
        kr_p_l.append(kr_p)
        lat_s_l.append(ckv_s)
        kr_s_l.append(kr_s)
        v_s_l.append(v_s.reshape(dec_b, dec_l, CHUNK_WIDTH))
        mk_p, mv_p = mem_kv(mem_prompt, norm_mem_kv[l], w_mem_k[l], w_mem_v[l])
        mk_p_l.append(mk_p)
        mv_p_l.append(mv_p)
        xp = xp + mem_attn(rmsnorm(xp, norm_mem_q[l]), mk_p, mv_p, w_mem_q[l], w_mem_o[l])
        xs = xs + mem_attn(rmsnorm(xs, norm_mem_q[l]), cache_mem_k[l], cache_mem_v[l], w_mem_q[l], w_mem_o[l])
        hp, hs = rmsnorm(xp, norm_ffn[l]), rmsnorm(xs, norm_ffn[l])
        i = l // 2
        if l % 2 == 0:
            xp = xp + swiglu(hp, w_gate_dense[i], w_up_dense[i], w_down_dense[i])
            xs = xs + swiglu(hs, w_gate_dense[i], w_up_dense[i], w_down_dense[i])
        else:
            xp = xp + moe(hp, w_router[i], w_gate_moe[i], w_up_moe[i], w_down_moe[i])
            xs = xs + moe(hs, w_router[i], w_gate_moe[i], w_up_moe[i], w_down_moe[i])
    y_prompt = rmsnorm(xp, final_norm)
    y_sample = rmsnorm(xs, final_norm)
    new_kv_latent_prompt = jnp.stack(lat_p_l)
    new_k_rope_prompt = jnp.stack(kr_p_l)
    new_mem_k_prompt = jnp.stack(mk_p_l)
    new_mem_v_prompt = jnp.stack(mv_p_l)
    new_kv_latent_sample = jnp.stack(lat_s_l)
    new_k_rope_sample = jnp.stack(kr_s_l)
    new_chunk_v_sample = jnp.stack(v_s_l)
    return (y_prompt, y_sample, new_kv_latent_prompt, new_k_rope_prompt, new_mem_k_prompt, new_mem_v_prompt,
            new_kv_latent_sample, new_k_rope_sample, new_chunk_v_sample)
```

```python
import functools

import jax
import jax.numpy as jnp
from jax import lax
from jax.experimental import pallas as pl
from jax.experimental.pallas import tpu as pltpu

F32 = jnp.float32
BF16 = jnp.bfloat16
U32 = jnp.uint32
I32 = jnp.int32

EPS = 1e-6
ROPE_THETA = 10000.0
ROPE = 64
NOPE = 128
LORA = 512
HEADS = 8
VDIM = 128
KCAT = 640
GROUPS = 8
CHUNK = 128
MEM_HEADS = 4
MEM_DIM = 128
N_EXPERTS = 8
MLA_SCALE = (NOPE + ROPE) ** -0.5
MEM_SCALE = MEM_DIM ** -0.5

LANES = 128
VMEM_CAP_BYTES = 60000 * 1024

TM = 1040
TQ = 256
PAGES_PER_STEP = 16
MEM_SAMPLES_PER_STEP = 8
MOE_TM = 512
ROW_TILE = 208


def _nbytes(shape, dtype):
    n = 1
    for s in shape:
        n *= s
    return n * jnp.dtype(dtype).itemsize


def _params(sem, blocks, scratch=0, temps=0):
    need = 2 * sum(_nbytes(s, d) for s, d in blocks) + scratch + temps
    return pltpu.CompilerParams(dimension_semantics=sem, vmem_limit_bytes=min(need, VMEM_CAP_BYTES))


def _rms(x, g):
    return x * lax.rsqrt(jnp.mean(x * x, axis=-1, keepdims=True) + EPS) * g


def _gelu(x):
    return x * (0.5 * (1.0 + jnp.tanh(0.7978845608028654 * (x + 0.044715 * (x * x * x)))))


def _sigmoid(x):
    return 1.0 / (1.0 + jnp.exp(-x))


def _dot(a, b):
    return jnp.dot(a, b, preferred_element_type=F32)


def _dot_nt(a, b):
    return lax.dot_general(a, b, (((1,), (1,)), ((), ())), preferred_element_type=F32)


def _in_proj_kernel(x_ref, g_ref, w_ref, wkr_ref, cg_ref, cos_ref, sin_ref,
                    z_ref, kr_ref, kcat_ref, hn_ref):
    j = pl.program_id(1)

    @pl.when(j == 0)
    def _():
        hn = _rms(x_ref[...], g_ref[...]).astype(BF16)
        hn_ref[...] = hn
        kr2 = _dot(hn, wkr_ref[...])
        kr = kr2[:, :ROPE] * cos_ref[...] + kr2[:, ROPE:] * sin_ref[...]
        kr_ref[...] = kr
        kcat_ref[:, LORA:KCAT] = jnp.concatenate([kr, jnp.zeros_like(kr)], axis=-1).astype(BF16)

    acc = _dot(hn_ref[...], w_ref[...])
    cg = cg_ref[...]

    @pl.when(j <= 1)
    def _():
        y = _rms(acc, cg)
        z_ref[...] = y

        @pl.when(j == 1)
        def _():
            kcat_ref[:, 0:LORA] = y.astype(BF16)

    @pl.when((j == 2) | (j == 3))
    def _():
        z_ref[...] = _gelu(acc)

    @pl.when(j >= 4)
    def _():
        gl = _gelu(acc)
        for k in range(LORA // CHUNK):
            sl = slice(k * CHUNK, (k + 1) * CHUNK)
            z_ref[:, sl] = _rms(gl[:, sl], cg[:, sl])


def _in_proj(x, g, w1, wkr, colgain, cos, sin):
    t, d = x.shape
    n = w1.shape[1]
    tn = LORA
    blocks = [((TM, d), F32), ((d, tn), BF16), ((d, 2 * ROPE), BF16), ((TM, tn), F32),
              ((TM, LANES), F32), ((TM, KCAT), BF16), ((TM, 2 * LANES), F32)]
    return pl.pallas_call(
        _in_proj_kernel,
        name="in_proj",
        grid=(t // TM, n // tn),
        in_specs=[
            pl.BlockSpec((TM, d), lambda i, j: (i, 0)),
            pl.BlockSpec((1, d), lambda i, j: (0, 0)),
            pl.BlockSpec((d, tn), lambda i, j: (0, j)),
            pl.BlockSpec((d, 2 * ROPE), lambda i, j: (0, 0)),
            pl.BlockSpec((1, tn), lambda i, j: (0, j)),
            pl.BlockSpec((TM, ROPE), lambda i, j: (i, 0)),
            pl.BlockSpec((TM, ROPE), lambda i, j: (i, 0)),
        ],
        out_specs=[
            pl.BlockSpec((TM, tn), lambda i, j: (i, j)),
            pl.BlockSpec((TM, ROPE), lambda i, j: (i, 0)),
            pl.BlockSpec((TM, KCAT), lambda i, j: (i, 0)),
        ],
        out_shape=[
            jax.ShapeDtypeStruct((t, n), F32),
            jax.ShapeDtypeStruct((t, ROPE), F32),
            jax.ShapeDtypeStruct((t, KCAT), BF16),
        ],
        scratch_shapes=[pltpu.VMEM((TM, d), BF16)],
        compiler_params=_params(("parallel", "arbitrary"), blocks,
                                scratch=_nbytes((TM, d), BF16), temps=3 * _nbytes((TM, d), F32)),
    )(x, g, w1, wkr, colgain, cos, sin)


def _q_proj_kernel(cq_ref, wuq_ref, wuk_ref, cos_ref, sin_ref, q_ref):
    cq = cq_ref[...].astype(BF16)
    q = _dot(cq, wuq_ref[...])
    qa = _dot(q[:, :NOPE].astype(BF16), wuk_ref[...])
    qr = q[:, NOPE:NOPE + ROPE] * cos_ref[...] + q[:, NOPE + ROPE:] * sin_ref[...]
    q_ref[:, 0:LORA] = (qa * MLA_SCALE).astype(BF16)
    q_ref[:, LORA:KCAT] = jnp.concatenate([qr * MLA_SCALE, jnp.zeros_like(qr)], axis=-1).astype(BF16)


def _q_proj(z, wuq2, wukt, cos, sin):
    t = z.shape[0]
    blocks = [((TM, LORA), F32), ((LORA, 2 * NOPE), BF16), ((NOPE, LORA), BF16),
              ((TM, LANES), F32), ((TM, LANES), F32), ((TM, KCAT), BF16)]
    return pl.pallas_call(
        _q_proj_kernel,
        name="q_proj",
        grid=(t // TM, HEADS),
        in_specs=[
            pl.BlockSpec((TM, LORA), lambda i, h: (i, 0)),
            pl.BlockSpec((None, LORA, 2 * NOPE), lambda i, h: (h, 0, 0)),
            pl.BlockSpec((None, NOPE, LORA), lambda i, h: (h, 0, 0)),
            pl.BlockSpec((TM, ROPE), lambda i, h: (i, 0)),
            pl.BlockSpec((TM, ROPE), lambda i, h: (i, 0)),
        ],
        out_specs=pl.BlockSpec((None, TM, KCAT), lambda i, h: (h, i, 0)),
        out_shape=jax.ShapeDtypeStruct((HEADS, t, KCAT), BF16),
        compiler_params=_params(("parallel", "arbitrary"), blocks, temps=4 * _nbytes((TM, LORA), F32)),
    )(z, wuq2, wukt, cos, sin)


def _attn_prompt_kernel(q_ref, k_ref, wuv_ref, o_ref, m_ref, l_ref, acc_ref, *, nq):
    step = pl.program_id(0)
    n_steps = pl.num_programs(0) - 1
    qi = step % nq
    rows = HEADS * TQ

    @pl.when(step < n_steps)
    def _():
        q = q_ref[...].reshape(rows, KCAT)
        m_ref[...] = jnp.full((rows, 1), -jnp.inf, F32)
        l_ref[...] = jnp.zeros((rows, 1), F32)
        acc_ref[...] = jnp.zeros((rows, LORA), F32)
        q_pos = lax.broadcasted_iota(I32, (rows, TQ), 0) & (TQ - 1)
        k_pos = lax.broadcasted_iota(I32, (rows, TQ), 1)

        def body(c, carry):
            kc = k_ref[pl.ds(pl.multiple_of(c * TQ, TQ), TQ), :]
            s = _dot_nt(q, kc)
            s = jnp.where((k_pos <= q_pos) | (c < qi), s, -jnp.inf)
            m_prev = m_ref[...]
            m_new = jnp.maximum(m_prev, jnp.max(s, axis=-1, keepdims=True))
            alpha = jnp.exp(m_prev - m_new)
            p = jnp.exp(s - m_new)
            l_ref[...] = alpha * l_ref[...] + jnp.sum(p, axis=-1, keepdims=True)
            acc_ref[...] = alpha * acc_ref[...] + _dot(p.astype(BF16), kc[:, :LORA])
            m_ref[...] = m_new
            return carry

        lax.fori_loop(0, qi + 1, body, 0)
        o = acc_ref[...] * (1.0 / l_ref[...])
        for h in range(HEADS):
            oh = o[h * TQ:(h + 1) * TQ, :].astype(BF16)
            o_ref[:, h * VDIM:(h + 1) * VDIM] = _dot(oh, wuv_ref[h])

    @pl.when(step == n_steps)
    def _():
        o_ref[...] = jnp.zeros(o_ref.shape, F32)


def _attn_prompt(q, kcat, wuv_h, batch, seq):
    t = kcat.shape[0]
    nq = seq // TQ
    n_steps = batch * nq
    rows = HEADS * TQ
    blocks = [((HEADS, TQ, KCAT), BF16), ((seq, KCAT), BF16), ((HEADS, LORA, VDIM), BF16),
              ((TQ, HEADS * VDIM), F32)]
    scratch = 2 * _nbytes((rows, LANES), F32) + _nbytes((rows, LORA), F32)
    return pl.pallas_call(
        functools.partial(_attn_prompt_kernel, nq=nq),
        name="attn_prompt",
        grid=(n_steps + 1,),
        in_specs=[
            pl.BlockSpec((HEADS, TQ, KCAT), lambda i: (0, jnp.minimum(i, n_steps - 1), 0)),
            pl.BlockSpec((seq, KCAT), lambda i: (jnp.minimum(i, n_steps - 1) // nq, 0)),
            pl.BlockSpec((HEADS, LORA, VDIM), lambda i: (0, 0, 0)),
        ],
        out_specs=pl.BlockSpec((TQ, HEADS * VDIM), lambda i: (i, 0)),
        out_shape=jax.ShapeDtypeStruct((t, HEADS * VDIM), F32),
        scratch_shapes=[pltpu.VMEM((rows, 1), F32), pltpu.VMEM((rows, 1), F32),
                        pltpu.VMEM((rows, LORA), F32)],
        compiler_params=_params(("arbitrary",), blocks, scratch=scratch,
                                temps=6 * _nbytes((rows, TQ), F32) + 2 * _nbytes((rows, LORA), F32)),
    )(q, kcat, wuv_h)


def _attn_sample_kernel(pt_ref, q_ref, ks_ref, wuv_ref, lat_hbm, kr_hbm, oin_ref, o_ref,
                        latbuf_ref, krbuf_ref, kbuf_ref, m_ref, l_ref, acc_ref, sem, *, layer):
    del oin_ref
    g_pages = PAGES_PER_STEP
    b = pl.program_id(0)
    s_id = pl.program_id(1)
    steps = pl.num_programs(1)
    page = lat_hbm.shape[2]
    n = b * steps + s_id
    slot = n % 2

    def page_copies(lin, dst_slot):
        copies = []
        for g in range(g_pages):
            pg = pt_ref[lin * g_pages + g]
            rows = pl.ds(g * page, page)
            copies.append(pltpu.make_async_copy(lat_hbm.at[layer, pg], latbuf_ref.at[dst_slot, rows, :],
                                                sem.at[0, dst_slot]))
            copies.append(pltpu.make_async_copy(kr_hbm.at[layer, pg], krbuf_ref.at[dst_slot, rows, :],
                                                sem.at[1, dst_slot]))
        return copies

    @pl.when(n == 0)
    def _():
        kbuf_ref[...] = jnp.zeros(kbuf_ref.shape, BF16)
        for c in page_copies(n, slot):
            c.start()

    @pl.when(n + 1 < pl.num_programs(0) * steps)
    def _():
        for c in page_copies(n + 1, 1 - slot):
            c.start()

    @pl.when(s_id == 0)
    def _():
        m_ref[...] = jnp.full((HEADS, 1), -jnp.inf, F32)
        l_ref[...] = jnp.zeros((HEADS, 1), F32)
        acc_ref[...] = jnp.zeros((HEADS, LORA), F32)

    for c in page_copies(n, slot):
        c.wait()
    kbuf_ref[:, 0:LORA] = latbuf_ref[slot].astype(BF16)
    kbuf_ref[:, LORA:LORA + ROPE] = krbuf_ref[slot].astype(BF16)

    q = q_ref[0]
    k = kbuf_ref[...]
    s = _dot_nt(q, k)
    m_prev = m_ref[...]
    m_new = jnp.maximum(m_prev, jnp.max(s, axis=-1, keepdims=True))
    alpha = jnp.exp(m_prev - m_new)
    p = jnp.exp(s - m_new)
    l_new = alpha * l_ref[...] + jnp.sum(p, axis=-1, keepdims=True)
    acc_new = alpha * acc_ref[...] + _dot(p.astype(BF16), k[:, :LORA])
    m_ref[...] = m_new
    l_ref[...] = l_new
    acc_ref[...] = acc_new

    @pl.when(s_id == pl.num_programs(1) - 1)
    def _():
        k_self = ks_ref[0].astype(F32)
        s_self = jnp.sum(q.astype(F32) * k_self, axis=-1, keepdims=True)
        m_fin = jnp.maximum(m_new, s_self)
        a = jnp.exp(m_new - m_fin)
        p_self = jnp.exp(s_self - m_fin)
        l_fin = a * l_new + p_self
        o_lat = (a * acc_new + p_self * k_self[:, :LORA]) * (1.0 / l_fin)
        r = _dot(o_lat.astype(BF16), wuv_ref[...])
        head = lax.broadcasted_iota(I32, r.shape, 0)
        col_head = lax.broadcasted_iota(I32, r.shape, 1) // VDIM
        o_ref[pl.ds(b, 1), :] = jnp.sum(jnp.where(head == col_head, r, 0.0), axis=0, keepdims=True)


def _attn_sample(layer, page_table, qs, ks, wuv_all, o_att, cache_lat, cache_kr):
    nb, n_pages = page_table.shape
    page = cache_lat.shape[2]
    g_pages = PAGES_PER_STEP
    steps = n_pages // g_pages
    t = o_att.shape[0]
    row_block = (t - nb) // nb
    pt_flat = page_table.reshape(-1)
    keys = g_pages * page
    blocks = [((HEADS, KCAT), BF16), ((1, KCAT), BF16), ((LORA, HEADS * VDIM), BF16),
              ((nb, HEADS * VDIM), F32)]
    scratch = (_nbytes((2, keys, LORA), F32) + _nbytes((2, keys, LANES), F32)
               + _nbytes((keys, KCAT), BF16) + 3 * _nbytes((HEADS, LORA), F32))
    grid_spec = pltpu.PrefetchScalarGridSpec(
        num_scalar_prefetch=1,
        grid=(nb, steps),
        in_specs=[
            pl.BlockSpec((1, HEADS, KCAT), lambda b, s, pt: (b, 0, 0)),
            pl.BlockSpec((1, 1, KCAT), lambda b, s, pt: (b, 0, 0)),
            pl.BlockSpec((LORA, HEADS * VDIM), lambda b, s, pt: (0, 0)),
            pl.BlockSpec(memory_space=pl.ANY),
            pl.BlockSpec(memory_space=pl.ANY),
            pl.BlockSpec(memory_space=pl.ANY),
        ],
        out_specs=pl.BlockSpec((nb, HEADS * VDIM), lambda b, s, pt: (row_block, 0)),
        scratch_shapes=[pltpu.VMEM((2, keys, LORA), F32), pltpu.VMEM((2, keys, ROPE), F32),
                        pltpu.VMEM((keys, KCAT), BF16), pltpu.VMEM((HEADS, 1), F32),
                        pltpu.VMEM((HEADS, 1), F32), pltpu.VMEM((HEADS, LORA), F32),
                        pltpu.SemaphoreType.DMA((2, 2))],
    )
    return pl.pallas_call(
        functools.partial(_attn_sample_kernel, layer=layer),
        name="attn_sample",
        grid_spec=grid_spec,
        out_shape=jax.ShapeDtypeStruct(o_att.shape, F32),
        input_output_aliases={6: 0},
        compiler_params=_params(("arbitrary", "arbitrary"), blocks, scratch=scratch,
                                temps=2 * _nbytes((keys, KCAT), BF16) + 8 * _nbytes((HEADS, keys), F32)),
    )(pt_flat, qs, ks, wuv_all, cache_lat, cache_kr, o_att)


def _chunk_prompt_kernel(u_ref, v_ref, ws_ref, bst_ref, o_ref):
    step = pl.program_id(0)
    n_steps = pl.num_programs(0) - 1

    @pl.when(step < n_steps)
    def _():
        row = lax.broadcasted_iota(I32, (CHUNK, CHUNK), 0)
        col = lax.broadcasted_iota(I32, (CHUNK, CHUNK), 1)
        n_chunks = u_ref.shape[0] // CHUNK
        for g in range(GROUPS):
            w = jnp.where(row >= col, ws_ref[g], 0.0).astype(BF16)
            bias = bst_ref[:, g:g + 1]
            cs = slice(g * CHUNK, (g + 1) * CHUNK)
            for n in range(n_chunks):
                rs = slice(n * CHUNK, (n + 1) * CHUNK)
                mixed = _dot(w, v_ref[rs, cs].astype(BF16)) + bias
                o_ref[rs, cs] = u_ref[rs, cs] * mixed

    @pl.when(step == n_steps)
    def _():
        o_ref[...] = jnp.zeros(o_ref.shape, F32)


def _chunk_prompt(z, ws, bst, n_prompt):
    t = z.shape[0]
    width = GROUPS * CHUNK
    rows = 4 * CHUNK
    n_steps = n_prompt // rows
    blocks = [((rows, width), F32)] * 3 + [((GROUPS, CHUNK, CHUNK), F32), ((CHUNK, LANES), F32)]
    return pl.pallas_call(
        _chunk_prompt_kernel,
        name="chunk_prompt",
        grid=(n_steps + 1,),
        in_specs=[
            pl.BlockSpec((rows, width), lambda i: (jnp.minimum(i, n_steps - 1), 1)),
            pl.BlockSpec((rows, width), lambda i: (jnp.minimum(i, n_steps - 1), 2)),
            pl.BlockSpec((GROUPS, CHUNK, CHUNK), lambda i: (0, 0, 0)),
            pl.BlockSpec((CHUNK, GROUPS), lambda i: (0, 0)),
        ],
        out_specs=pl.BlockSpec((rows, width), lambda i: (i, 0)),
        out_shape=jax.ShapeDtypeStruct((t, width), F32),
        compiler_params=_params(("arbitrary",), blocks, temps=_nbytes((rows, width), F32)),
    )(z, z, ws, bst)


def _chunk_sample_kernel(u_ref, v_ref, a_ref, c_ref, oin_ref, o_ref):
    del oin_ref
    o_ref[...] = u_ref[...] * (a_ref[...] * v_ref[...] + c_ref[...])


def _chunk_sample(z, a, c, o_chk, n_sample):
    t = z.shape[0]
    width = GROUPS * CHUNK
    rb = (t - n_sample) // n_sample
    blocks = [((n_sample, width), F32)] * 3 + [((8, width), F32)] * 2
    return pl.pallas_call(
        _chunk_sample_kernel,
        name="chunk_sample",
        grid=(1,),
        in_specs=[
            pl.BlockSpec((n_sample, width), lambda i: (rb, 1)),
            pl.BlockSpec((n_sample, width), lambda i: (rb, 2)),
            pl.BlockSpec((1, width), lambda i: (0, 0)),
            pl.BlockSpec((1, width), lambda i: (0, 0)),
            pl.BlockSpec(memory_space=pl.ANY),
        ],
        out_specs=pl.BlockSpec((n_sample, width), lambda i: (rb, 0)),
        out_shape=jax.ShapeDtypeStruct(o_chk.shape, F32),
        input_output_aliases={4: 0},
        compiler_params=_params(("arbitrary",), blocks),
    )(z, z, a, c, o_chk)


def _out_proj_kernel(oa_ref, oc_ref, ga_ref, gc_ref, w_ref, x_ref, o_ref, cat_ref):
    half = oa_ref.shape[1]

    @pl.when(pl.program_id(1) == 0)
    def _():
        cat_ref[:, 0:half] = _rms(oa_ref[...], ga_ref[...]).astype(BF16)
        cat_ref[:, half:] = _rms(oc_ref[...], gc_ref[...]).astype(BF16)

    o_ref[...] = x_ref[...] + _dot(cat_ref[...], w_ref[...])


def _out_proj(o_att, o_chk, ga, gc, w, x):
    t, d = x.shape
    half = o_att.shape[1]
    tn = 512
    blocks = [((TM, half), F32)] * 2 + [((d, tn), BF16), ((TM, tn), F32), ((TM, tn), F32)]
    return pl.pallas_call(
        _out_proj_kernel,
        name="out_proj",
        grid=(t // TM, d // tn),
        in_specs=[
            pl.BlockSpec((TM, half), lambda i, j: (i, 0)),
            pl.BlockSpec((TM, half), lambda i, j: (i, 0)),
            pl.BlockSpec((1, half), lambda i, j: (0, 0)),
            pl.BlockSpec((1, half), lambda i, j: (0, 0)),
            pl.BlockSpec((d, tn), lambda i, j: (0, j)),
            pl.BlockSpec((TM, tn), lambda i, j: (i, j)),
        ],
        out_specs=pl.BlockSpec((TM, tn), lambda i, j: (i, j)),
        out_shape=jax.ShapeDtypeStruct((t, d), F32),
        scratch_shapes=[pltpu.VMEM((TM, d), BF16)],
        compiler_params=_params(("parallel", "arbitrary"), blocks, scratch=_nbytes((TM, d), BF16),
                                temps=3 * _nbytes((TM, half), F32)),
    )(o_att, o_chk, ga, gc, w, x)


def _norm_mm_kernel(x_ref, g_ref, w_ref, o_ref, hn_ref):
    @pl.when(pl.program_id(1) == 0)
    def _():
        hn_ref[...] = _rms(x_ref[...], g_ref[...]).astype(BF16)

    o_ref[...] = _dot(hn_ref[...], w_ref[...])


def _norm_mm(x, g, w, tm, tn):
    m, d = x.shape
    n = w.shape[1]
    blocks = [((tm, d), F32), ((d, tn), BF16), ((tm, tn), F32)]
    return pl.pallas_call(
        _norm_mm_kernel,
        name="mem_kv",
        grid=(m // tm, n // tn),
        in_specs=[
            pl.BlockSpec((tm, d), lambda i, j: (i, 0)),
            pl.BlockSpec((1, d), lambda i, j: (0, 0)),
            pl.BlockSpec((d, tn), lambda i, j: (0, j)),
        ],
        out_specs=pl.BlockSpec((tm, tn), lambda i, j: (i, j)),
        out_shape=jax.ShapeDtypeStruct((m, n), F32),
        scratch_shapes=[pltpu.VMEM((tm, d), BF16)],
        compiler_params=_params(("parallel", "arbitrary"), blocks, scratch=_nbytes((tm, d), BF16),
                                temps=3 * _nbytes((tm, d), F32)),
    )(x, g, w)


def _mem_prompt_kernel(x_ref, g_ref, wq_ref, kv_ref, wo_ref, o_ref):
    step = pl.program_id(0)
    n_steps = pl.num_programs(0) - 1

    @pl.when(step < n_steps)
    def _():
        x = x_ref[...]
        q = _dot(_rms(x, g_ref[...]).astype(BF16), wq_ref[...])
        width = MEM_HEADS * MEM_DIM
        outs = []
        for h in range(MEM_HEADS):
            hs = slice(h * MEM_DIM, (h + 1) * MEM_DIM)
            vs = slice(width + h * MEM_DIM, width + (h + 1) * MEM_DIM)
            s = _dot_nt(q[:, hs].astype(BF16), kv_ref[:, hs].astype(BF16)) * MEM_SCALE
            p = jnp.exp(s - jnp.max(s, axis=-1, keepdims=True))
            p = p * (1.0 / jnp.sum(p, axis=-1, keepdims=True))
            outs.append(_dot(p.astype(BF16), kv_ref[:, vs].astype(BF16)))
        o = jnp.concatenate(outs, axis=-1).astype(BF16)
        o_ref[...] = x + _dot(o, wo_ref[...])

    @pl.when(step == n_steps)
    def _():
        o_ref[...] = jnp.zeros(o_ref.shape, F32)


def _mem_prompt(x, g, wq, mkv, wo, n_prompt, seq, mem_tokens):
    t, d = x.shape
    rows = 512
    width = MEM_HEADS * MEM_DIM
    per_batch = seq // rows
    n_steps = n_prompt // rows
    blocks = [((rows, d), F32), ((d, width), BF16), ((mem_tokens, 2 * width), F32),
              ((width, d), BF16), ((rows, d), F32)]
    return pl.pallas_call(
        _mem_prompt_kernel,
        name="mem_prompt",
        grid=(n_steps + 1,),
        in_specs=[
            pl.BlockSpec((rows, d), lambda i: (jnp.minimum(i, n_steps - 1), 0)),
            pl.BlockSpec((1, d), lambda i: (0, 0)),
            pl.BlockSpec((d, width), lambda i: (0, 0)),
            pl.BlockSpec((mem_tokens, 2 * width), lambda i: (jnp.minimum(i, n_steps - 1) // per_batch, 0)),
            pl.BlockSpec((width, d), lambda i: (0, 0)),
        ],
        out_specs=pl.BlockSpec((rows, d), lambda i: (i, 0)),
        out_shape=jax.ShapeDtypeStruct((t, d), F32),
        compiler_params=_params(("arbitrary",), blocks, temps=4 * _nbytes((rows, d), F32)),
    )(x, g, wq, mkv, wo)


def _mem_sample_kernel(x_ref, g_ref, wq_ref, k_ref, v_ref, wo_ref, xin_ref, o_ref, q_scr, o_scr):
    del xin_ref
    step = pl.program_id(0)
    width = MEM_HEADS * MEM_DIM

    @pl.when(step == 0)
    def _():
        q_scr[...] = _dot(_rms(x_ref[...], g_ref[...]).astype(BF16), wq_ref[...])

    head = lax.broadcasted_iota(I32, (8, width), 0)
    col_head = lax.broadcasted_iota(I32, (8, width), 1) // MEM_DIM
    own = head == col_head
    for n in range(MEM_SAMPLES_PER_STEP):
        r = step * MEM_SAMPLES_PER_STEP + n
        q_bd = jnp.where(own, q_scr[pl.ds(r, 1), :], 0.0).astype(BF16)
        s = _dot_nt(q_bd, k_ref[n].astype(BF16)) * MEM_SCALE
        p = jnp.exp(s - jnp.max(s, axis=-1, keepdims=True))
        p = p * (1.0 / jnp.sum(p, axis=-1, keepdims=True))
        ob = _dot(p.astype(BF16), v_ref[n].astype(BF16))
        o_scr[pl.ds(r, 1), :] = jnp.sum(jnp.where(own, ob, 0.0), axis=0, keepdims=True)

    @pl.when(step == pl.num_programs(0) - 1)
    def _():
        o_ref[...] = x_ref[...] + _dot(o_scr[...].astype(BF16), wo_ref[...])


def _mem_sample(layer, x_mid, x_new, g, wq, wo, cache_k, cache_v, n_sample):
    t, d = x_mid.shape
    width = MEM_HEADS * MEM_DIM
    mem_tokens = cache_k.shape[2]
    per = MEM_SAMPLES_PER_STEP
    rb = (t - n_sample) // n_sample
    blocks = [((n_sample, d), F32), ((d, width), BF16), ((per, mem_tokens, width), F32),
              ((per, mem_tokens, width), F32), ((width, d), BF16), ((n_sample, d), F32)]
    return pl.pallas_call(
        _mem_sample_kernel,
        name="mem_sample",
        grid=(n_sample // per,),
        in_specs=[
            pl.BlockSpec((n_sample, d), lambda i: (rb, 0)),
            pl.BlockSpec((1, d), lambda i: (0, 0)),
            pl.BlockSpec((d, width), lambda i: (0, 0)),
            pl.BlockSpec((None, per, mem_tokens, width), lambda i: (layer, i, 0, 0)),
            pl.BlockSpec((None, per, mem_tokens, width), lambda i: (layer, i, 0, 0)),
            pl.BlockSpec((width, d), lambda i: (0, 0)),
            pl.BlockSpec(memory_space=pl.ANY),
        ],
        out_specs=pl.BlockSpec((n_sample, d), lambda i: (rb, 0)),
        out_shape=jax.ShapeDtypeStruct((t, d), F32),
        scratch_shapes=[pltpu.VMEM((n_sample, width), F32), pltpu.VMEM((n_sample, width), F32)],
        input_output_aliases={6: 0},
        compiler_params=_params(("arbitrary",), blocks, scratch=2 * _nbytes((n_sample, width), F32),
                                temps=4 * _nbytes((n_sample, d), F32)),
    )(x_mid, g, wq, cache_k, cache_v, wo, x_new)


def _ffn_up_kernel(x_ref, g_ref, wg_ref, wu_ref, o_ref, hn_ref):
    @pl.when(pl.program_id(1) == 0)
    def _():
        hn_ref[...] = _rms(x_ref[...], g_ref[...]).astype(BF16)

    hn = hn_ref[...]
    a = _dot(hn, wg_ref[...])
    b = _dot(hn, wu_ref[...])
    o_ref[...] = (a * _sigmoid(a) * b).astype(BF16)


def _ffn_up(x, g, wg, wu):
    t, d = x.shape
    f = wg.shape[1]
    tn = 512
    blocks = [((TM, d), F32), ((d, tn), BF16), ((d, tn), BF16), ((TM, tn), BF16)]
    return pl.pallas_call(
        _ffn_up_kernel,
        name="ffn_up",
        grid=(t // TM, f // tn),
        in_specs=[
            pl.BlockSpec((TM, d), lambda i, j: (i, 0)),
            pl.BlockSpec((1, d), lambda i, j: (0, 0)),
            pl.BlockSpec((d, tn), lambda i, j: (0, j)),
            pl.BlockSpec((d, tn), lambda i, j: (0, j)),
        ],
        out_specs=pl.BlockSpec((TM, tn), lambda i, j: (i, j)),
        out_shape=jax.ShapeDtypeStruct((t, f), BF16),
        scratch_shapes=[pltpu.VMEM((TM, d), BF16)],
        compiler_params=_params(("parallel", "arbitrary"), blocks, scratch=_nbytes((TM, d), BF16),
                                temps=3 * _nbytes((TM, d), F32)),
    )(x, g, wg, wu)


def _mm_res_kernel(a_ref, w_ref, x_ref, o_ref):
    o_ref[...] = x_ref[...] + _dot(a_ref[...], w_ref[...])


def _ffn_down(act, wd, x):
    t, f = act.shape
    d = wd.shape[1]
    tm, tn = 640, 512
    blocks = [((tm, f), BF16), ((f, tn), BF16), ((tm, tn), F32), ((tm, tn), F32)]
    return pl.pallas_call(
        _mm_res_kernel,
        name="ffn_down",
        grid=(t // tm, d // tn),
        in_specs=[
            pl.BlockSpec((tm, f), lambda i, j: (i, 0)),
            pl.BlockSpec((f, tn), lambda i, j: (0, j)),
            pl.BlockSpec((tm, tn), lambda i, j: (i, j)),
        ],
        out_specs=pl.BlockSpec((tm, tn), lambda i, j: (i, j)),
        out_shape=jax.ShapeDtypeStruct((t, d), F32),
        compiler_params=_params(("parallel", "arbitrary"), blocks, temps=2 * _nbytes((tm, tn), F32)),
    )(act, wd, x)


def _router_kernel(x_ref, g_ref, whi_ref, wlo_ref, idx_ref, wgt_ref):
    h = _rms(x_ref[...], g_ref[...])
    h_hi = h.astype(BF16)
    h_lo = (h - h_hi.astype(F32)).astype(BF16)
    logits = _dot(h_hi, whi_ref[...]) + (_dot(h_lo, whi_ref[...]) + _dot(h_hi, wlo_ref[...]))
    lane = lax.broadcasted_iota(I32, logits.shape, 1)
    logits = jnp.where(lane < N_EXPERTS, logits, -jnp.inf)
    v1 = jnp.max(logits, axis=-1, keepdims=True)
    i1 = jnp.min(jnp.where(logits == v1, lane, LANES), axis=-1, keepdims=True)
    rest = jnp.where(lane == i1, -jnp.inf, logits)
    v2 = jnp.max(rest, axis=-1, keepdims=True)
    i2 = jnp.min(jnp.where(rest == v2, lane, LANES), axis=-1, keepdims=True)
    e2 = jnp.exp(v2 - v1)
    w1 = 1.0 / (1.0 + e2)
    w2 = e2 * w1
    idx_ref[...] = jnp.where(lane == 0, i1, jnp.where(lane == 1, i2, 0))
    wgt_ref[...] = jnp.where(lane == 0, w1, jnp.where(lane == 1, w2, 0.0))


def _router(x, g, whi, wlo):
    t, d = x.shape
    blocks = [((TM, d), F32), ((d, LANES), BF16), ((d, LANES), BF16), ((TM, LANES), I32), ((TM, LANES), F32)]
    return pl.pallas_call(
        _router_kernel,
        name="router",
        grid=(t // TM,),
        in_specs=[
            pl.BlockSpec((TM, d), lambda i: (i, 0)),
            pl.BlockSpec((1, d), lambda i: (0, 0)),
            pl.BlockSpec((d, LANES), lambda i: (0, 0)),
            pl.BlockSpec((d, LANES), lambda i: (0, 0)),
        ],
        out_specs=[pl.BlockSpec((TM, LANES), lambda i: (i, 0)), pl.BlockSpec((TM, LANES), lambda i: (i, 0))],
        out_shape=[jax.ShapeDtypeStruct((t, LANES), I32), jax.ShapeDtypeStruct((t, LANES), F32)],
        compiler_params=_params(("parallel",), blocks, temps=5 * _nbytes((TM, d), F32)),
    )(x, g, whi, wlo)


def _dispatch_kernel(pos_ref, x_ref, g_ref, xs_in_ref, xs_ref, buf_ref, sem):
    del xs_in_ref
    i = pl.program_id(0)
    rows, d = x_ref.shape
    half = d // 2
    h = _rms(x_ref[...], g_ref[...]).astype(BF16).astype(F32)
    lo = pltpu.bitcast(h[:, :half], U32) >> jnp.uint32(16)
    hi = pltpu.bitcast(h[:, half:], U32)
    buf_ref[...] = lo | hi

    def row_copy(r, k):
        dst = pos_ref[2 * (i * rows + r) + k]
        return pltpu.make_async_copy(buf_ref.at[pl.ds(r, 1), :], xs_ref.at[pl.ds(dst, 1), :], sem)

    def start(r, carry):
        row_copy(r, 0).start()
        row_copy(r, 1).start()
        return carry

    def wait(r, carry):
        row_copy(r, 0).wait()
        row_copy(r, 1).wait()
        return carry

    lax.fori_loop(0, rows, start, 0)
    lax.fori_loop(0, rows, wait, 0)


def _dispatch(pos, x, g, xs_init):
    t, d = x.shape
    blocks = [((ROW_TILE, d), F32)]
    grid_spec = pltpu.PrefetchScalarGridSpec(
        num_scalar_prefetch=1,
        grid=(t // ROW_TILE,),
        in_specs=[
            pl.BlockSpec((ROW_TILE, d), lambda i, pos: (i, 0)),
            pl.BlockSpec((1, d), lambda i, pos: (0, 0)),
            pl.BlockSpec(memory_space=pl.ANY),
        ],
        out_specs=pl.BlockSpec(memory_space=pl.ANY),
        scratch_shapes=[pltpu.VMEM((ROW_TILE, d // 2), U32), pltpu.SemaphoreType.DMA],
    )
    return pl.pallas_call(
        _dispatch_kernel,
        name="moe_dispatch",
        grid_spec=grid_spec,
        out_shape=jax.ShapeDtypeStruct(xs_init.shape, U32),
        input_output_aliases={3: 0},
        compiler_params=_params(("arbitrary",), blocks, scratch=_nbytes((ROW_TILE, d // 2), U32),
                                temps=4 * _nbytes((ROW_TILE, d), F32)),
    )(pos, x, g, xs_init)


def _unpack_pairs(u):
    lo = pltpu.bitcast(u << jnp.uint32(16), F32).astype(BF16)
    hi = pltpu.bitcast(u & jnp.uint32(0xFFFF0000), F32).astype(BF16)
    return lo, hi


def _moe_up_kernel(te_ref, nt_ref, x_ref, wg_ref, wu_ref, o_ref):
    del te_ref
    i = pl.program_id(1)
    half = x_ref.shape[1]

    @pl.when(i < nt_ref[0])
    def _():
        lo, hi = _unpack_pairs(x_ref[...])
        a = _dot(lo, wg_ref[0:half, :]) + _dot(hi, wg_ref[half:, :])
        b = _dot(lo, wu_ref[0:half, :]) + _dot(hi, wu_ref[half:, :])
        o_ref[...] = (a * _sigmoid(a) * b).astype(BF16)

    @pl.when(i >= nt_ref[0])
    def _():
        o_ref[...] = jnp.zeros(o_ref.shape, BF16)


def _moe_up(tile_expert, n_tiles, xs, wg, wu):
    r, half = xs.shape
    d, f = wg.shape[1], wg.shape[2]
    tf = f // 2
    blocks = [((MOE_TM, half), U32), ((d, tf), BF16), ((d, tf), BF16), ((MOE_TM, tf), BF16)]
    grid_spec = pltpu.PrefetchScalarGridSpec(
        num_scalar_prefetch=2,
        grid=(f // tf, r // MOE_TM),
        in_specs=[
            pl.BlockSpec((MOE_TM, half), lambda j, i, te, nt: (i, 0)),
            pl.BlockSpec((None, d, tf), lambda j, i, te, nt: (te[i], 0, j)),
            pl.BlockSpec((None, d, tf), lambda j, i, te, nt: (te[i], 0, j)),
        ],
        out_specs=pl.BlockSpec((MOE_TM, tf), lambda j, i, te, nt: (i, j)),
    )
    return pl.pallas_call(
        _moe_up_kernel,
        name="moe_up",
        grid_spec=grid_spec,
        out_shape=jax.ShapeDtypeStruct((r, f), BF16),
        compiler_params=_params(("arbitrary", "arbitrary"), blocks, temps=6 * _nbytes((MOE_TM, tf), F32)),
    )(tile_expert, n_tiles, xs, wg, wu)


def _moe_down_kernel(te_ref, nt_ref, a_ref, w_ref, o_ref):
    del te_ref
    i = pl.program_id(1)

    @pl.when(i < nt_ref[0])
    def _():
        o_ref[...] = _dot(a_ref[...], w_ref[...])

    @pl.when(i >= nt_ref[0])
    def _():
        o_ref[...] = jnp.zeros(o_ref.shape, F32)


def _moe_down(tile_expert, n_tiles, act, wd):
    r, f = act.shape
    d = wd.shape[2]
    tn = d // 2
    blocks = [((MOE_TM, f), BF16), ((f, tn), BF16), ((MOE_TM, tn), F32)]
    grid_spec = pltpu.PrefetchScalarGridSpec(
        num_scalar_prefetch=2,
        grid=(d // tn, r // MOE_TM),
        in_specs=[
            pl.BlockSpec((MOE_TM, f), lambda j, i, te, nt: (i, 0)),
            pl.BlockSpec((None, f, tn), lambda j, i, te, nt: (te[i], 0, j)),
        ],
        out_specs=pl.BlockSpec((MOE_TM, tn), lambda j, i, te, nt: (i, j)),
    )
    return pl.pallas_call(
        _moe_down_kernel,
        name="moe_down",
        grid_spec=grid_spec,
        out_shape=jax.ShapeDtypeStruct((r, d), F32),
        compiler_params=_params(("arbitrary", "arbitrary"), blocks, temps=2 * _nbytes((MOE_TM, tn), F32)),
    )(tile_expert, n_tiles, act, wd)


def _combine_kernel(pos_ref, x_ref, wgt_ref, y_ref, o_ref, buf_ref, sem):
    i = pl.program_id(0)
    rows = x_ref.shape[0]

    def row_copy(r, k):
        src = pos_ref[2 * (i * rows + r) + k]
        return pltpu.make_async_copy(y_ref.at[pl.ds(src, 1), :], buf_ref.at[k, pl.ds(r, 1), :], sem)

    def start(r, carry):
        row_copy(r, 0).start()
        row_copy(r, 1).start()
        return carry

    def wait(r, carry):
        row_copy(r, 0).wait()
        row_copy(r, 1).wait()
        return carry

    lax.fori_loop(0, rows, start, 0)
    lax.fori_loop(0, rows, wait, 0)
    w = wgt_ref[...]
    o_ref[...] = x_ref[...] + (w[:, 0:1] * buf_ref[0] + w[:, 1:2] * buf_ref[1])


def _combine(pos, x, wgt, y):
    t, d = x.shape
    blocks = [((ROW_TILE, d), F32), ((ROW_TILE, LANES), F32), ((ROW_TILE, d), F32)]
    grid_spec = pltpu.PrefetchScalarGridSpec(
        num_scalar_prefetch=1,
        grid=(t // ROW_TILE,),
        in_specs=[
            pl.BlockSpec((ROW_TILE, d), lambda i, pos: (i, 0)),
            pl.BlockSpec((ROW_TILE, LANES), lambda i, pos: (i, 0)),
            pl.BlockSpec(memory_space=pl.ANY),
        ],
        out_specs=pl.BlockSpec((ROW_TILE, d), lambda i, pos: (i, 0)),
        scratch_shapes=[pltpu.VMEM((2, ROW_TILE, d), F32), pltpu.SemaphoreType.DMA],
    )
    return pl.pallas_call(
        _combine_kernel,
        name="moe_combine",
        grid_spec=grid_spec,
        out_shape=jax.ShapeDtypeStruct((t, d), F32),
        compiler_params=_params(("arbitrary",), blocks, scratch=_nbytes((2, ROW_TILE, d), F32),
                                temps=2 * _nbytes((ROW_TILE, d), F32)),
    )(pos, x, wgt, y)


def _route_plan(idx, n_rows_pad):
    flat_e = idx.reshape(-1)
    onehot = (flat_e[:, None] == jnp.arange(N_EXPERTS, dtype=I32)[None, :]).astype(I32)
    csum = jnp.cumsum(onehot, axis=0)
    counts = csum[-1]
    rank = jnp.sum(onehot * (csum - 1), axis=1)
    tiles_per = (counts + MOE_TM - 1) // MOE_TM
    ends = jnp.cumsum(tiles_per)
    row_start = (ends - tiles_per) * MOE_TM
    pos = jnp.sum(onehot * row_start[None, :], axis=1) + rank
    n_tiles = ends[-1]
    tile_ids = jnp.arange(n_rows_pad // MOE_TM, dtype=I32)
    tile_expert = jnp.sum((tile_ids[:, None] >= ends[None, :]).astype(I32), axis=1)
    last_expert = jnp.sum((n_tiles - 1 >= ends).astype(I32))
    tile_expert = jnp.minimum(tile_expert, last_expert)
    return pos.astype(I32), tile_expert.astype(I32), n_tiles.reshape(1).astype(I32)


def _moe(x, g, whi, wlo, wg, wu, wd):
    t, d = x.shape
    idx, wgt = _router(x, g, whi, wlo)
    n_pairs = 2 * t
    n_rows_pad = ((n_pairs + N_EXPERTS * (MOE_TM - 1)) // MOE_TM) * MOE_TM
    pos, tile_expert, n_tiles = _route_plan(idx[:, :2], n_rows_pad)
    xs = _dispatch(pos, x, g, jnp.zeros((n_rows_pad, d // 2), U32))
    act = _moe_up(tile_expert, n_tiles, xs, wg, wu)
    y = _moe_down(tile_expert, n_tiles, act, wd)
    return _combine(pos, x, wgt, y)


def _final_norm_kernel(x_ref, g_ref, o_ref):
    o_ref[...] = _rms(x_ref[...], g_ref[...])


def _final_norm(x, g):
    t, d = x.shape
    blocks = [((TM, d), F32), ((TM, d), F32)]
    return pl.pallas_call(
        _final_norm_kernel,
        name="final_norm",
        grid=(t // TM,),
        in_specs=[pl.BlockSpec((TM, d), lambda i: (i, 0)), pl.BlockSpec((1, d), lambda i: (0, 0))],
        out_specs=pl.BlockSpec((TM, d), lambda i: (i, 0)),
        out_shape=jax.ShapeDtypeStruct((t, d), F32),
        compiler_params=_params(("parallel",), blocks, temps=2 * _nbytes((TM, d), F32)),
    )(x, g)


def _rotate_half_cols(w):
    half = w.shape[-1] // 2
    return jnp.concatenate([-w[..., half:], w[..., :half]], axis=-1)


def kernel(x_prompt, x_sample, mem_prompt, cache_kv_latent, cache_k_rope, cache_mem_k, cache_mem_v, page_table, norm_mix, w_in, q_norm, w_uq, kv_norm, w_uk, w_uv, chunk_v_norm, w_spatial, b_spatial, out_norm_mla, out_norm_chunk, w_out, norm_mem_q, norm_mem_kv, w_mem_q, w_mem_k, w_mem_v, w_mem_o, norm_ffn, w_gate_dense, w_up_dense, w_down_dense, w_router, w_gate_moe, w_up_moe, w_down_moe, final_norm):
    batch, seq, d = x_prompt.shape
    n_sample = x_sample.shape[0]
    depth = w_in.shape[0]
    mem_tokens = mem_prompt.shape[1]
    n_prompt = batch * seq
    t = n_prompt + n_sample
    past_len = page_table.shape[1] * cache_kv_latent.shape[2]
    mem_width = MEM_HEADS * MEM_DIM
    width = GROUPS * CHUNK

    o_kv, o_kr, o_u = LORA, 2 * LORA, 2 * LORA + ROPE
    w1 = jnp.concatenate([w_in[:, :, :o_kr], w_in[:, :, o_u:]], axis=2).astype(BF16)
    w_kr = w_in[:, :, o_kr:o_u]
    wkr2 = jnp.concatenate([w_kr, _rotate_half_cols(w_kr)], axis=2).astype(BF16)
    colgain = jnp.concatenate([q_norm, kv_norm, jnp.ones((depth, width), F32),
                               chunk_v_norm.reshape(depth, width)], axis=1).reshape(depth, 1, -1)
    q_rope_w = w_uq[..., NOPE:]
    wuq2 = jnp.concatenate([w_uq[..., :NOPE], q_rope_w, _rotate_half_cols(q_rope_w)], axis=-1)
    wuq2 = wuq2.transpose(0, 2, 1, 3).astype(BF16)
    wukt = w_uk.transpose(0, 2, 3, 1).astype(BF16)
    wuv_h = w_uv.transpose(0, 2, 1, 3).astype(BF16)
    wuv_all = w_uv.reshape(depth, LORA, HEADS * VDIM).astype(BF16)
    wout = w_out.astype(BF16)
    wmq = w_mem_q.reshape(depth, d, mem_width).astype(BF16)
    wmkv = jnp.concatenate([w_mem_k.reshape(depth, d, mem_width),
                            w_mem_v.reshape(depth, d, mem_width)], axis=2).astype(BF16)
    wmo = w_mem_o.reshape(depth, mem_width, d).astype(BF16)
    wgd, wud, wdd = w_gate_dense.astype(BF16), w_up_dense.astype(BF16), w_down_dense.astype(BF16)
    wgm, wum, wdm = w_gate_moe.astype(BF16), w_up_moe.astype(BF16), w_down_moe.astype(BF16)
    wr = jnp.pad(w_router, ((0, 0), (0, 0), (0, LANES - N_EXPERTS)))
    wr_hi = wr.astype(BF16)
    wr_lo = (wr - wr_hi.astype(F32)).astype(BF16)
    bst = b_spatial.transpose(0, 2, 1)
    mix_a = jnp.repeat(w_spatial[:, :, 0, 0], CHUNK, axis=1).reshape(depth, 1, width)
    mix_c = jnp.repeat(b_spatial[:, :, 0], CHUNK, axis=1).reshape(depth, 1, width)
    cache_k = cache_mem_k.reshape(depth, n_sample, mem_tokens, mem_width)
    cache_v = cache_mem_v.reshape(depth, n_sample, mem_tokens, mem_width)

    pos = jnp.concatenate([jnp.tile(jnp.arange(seq), batch), jnp.full((n_sample,), past_len)])
    inv_freq = 1.0 / (ROPE_THETA ** (jnp.arange(ROPE // 2, dtype=F32) / (ROPE // 2)))
    ang = pos.astype(F32)[:, None] * inv_freq[None, :]
    cos = jnp.tile(jnp.cos(ang), (1, 2))
    sin = jnp.tile(jnp.sin(ang), (1, 2))

    def gain(v):
        return v.reshape(1, -1)

    x = jnp.concatenate([x_prompt.reshape(n_prompt, d), x_sample.reshape(n_sample, d)], axis=0)
    mem_flat = mem_prompt.reshape(batch * mem_tokens, d)
    lat_p, kr_p, mk_p, mv_p, lat_s, kr_s, v_s = [], [], [], [], [], [], []
    for l in range(depth):
        z, kr, kcat = _in_proj(x, gain(norm_mix[l]), w1[l], wkr2[l], colgain[l], cos, sin)
        q = _q_proj(z, wuq2[l], wukt[l], cos, sin)
        o_att = _attn_prompt(q, kcat, wuv_h[l], batch, seq)
        qs = q[:, n_prompt:, :].transpose(1, 0, 2)
        ks = kcat[n_prompt:].reshape(n_sample, 1, KCAT)
        o_att = _attn_sample(l, page_table, qs, ks, wuv_all[l], o_att, cache_kv_latent, cache_k_rope)
        o_chk = _chunk_prompt(z, w_spatial[l], bst[l], n_prompt)
        o_chk = _chunk_sample(z, mix_a[l], mix_c[l], o_chk, n_sample)
        x = _out_proj(o_att, o_chk, gain(out_norm_mla[l]), gain(out_norm_chunk[l]), wout[l], x)
        lat_p.append(z[:n_prompt, o_kv:o_kr].reshape(batch, seq, LORA))
        kr_p.append(kr[:n_prompt].reshape(batch, seq, ROPE))
        lat_s.append(z[n_prompt:, o_kv:o_kr].reshape(n_sample, 1, LORA))
        kr_s.append(kr[n_prompt:].reshape(n_sample, 1, ROPE))
        v_s.append(z[n_prompt:, 2 * LORA + width:].reshape(n_sample, 1, width))

        mkv = _norm_mm(mem_flat, gain(norm_mem_kv[l]), wmkv[l], 512, 512)
        mk_p.append(mkv[:, :mem_width].reshape(batch, mem_tokens, MEM_HEADS, MEM_DIM))
        mv_p.append(mkv[:, mem_width:].reshape(batch, mem_tokens, MEM_HEADS, MEM_DIM))
        x_new = _mem_prompt(x, gain(norm_mem_q[l]), wmq[l], mkv, wmo[l], n_prompt, seq, mem_tokens)
        x = _mem_sample(l, x, x_new, gain(norm_mem_q[l]), wmq[l], wmo[l], cache_k, cache_v, n_sample)

        i = l // 2
        if l % 2 == 0:
            act = _ffn_up(x, gain(norm_ffn[l]), wgd[i], wud[i])
            x = _ffn_down(act, wdd[i], x)
        else:
            x = _moe(x, gain(norm_ffn[l]), wr_hi[i], wr_lo[i], wgm[i], wum[i], wdm[i])

    y = _final_norm(x, gain(final_norm))
    return (y[:n_prompt].reshape(batch, seq, d), y[n_prompt:].reshape(n_sample, 1, d),
            jnp.stack(lat_p), jnp.stack(kr_p), jnp.stack(mk_p), jnp.stack(mv_p),
            jnp.stack(lat_s), jnp.stack(kr_s), jnp.stack(v_s))
```

```python
import functools

import jax
import jax.numpy as jnp
from jax import lax
from jax.experimental import pallas as pl
from jax.experimental.pallas import tpu as pltpu

F32 = jnp.float32
BF16 = jnp.bfloat16
U32 = jnp.uint32
I32 = jnp.int32

EPS = 1e-6
ROPE_THETA = 10000.0
ROPE = 64
NOPE = 128
LORA = 512
HEADS = 8
VDIM = 128
KCAT = 640
GROUPS = 8
CHUNK = 128
MEM_HEADS = 4
MEM_DIM = 128
N_EXPERTS = 8
MLA_SCALE = (NOPE + ROPE) ** -0.5
MEM_SCALE = MEM_DIM ** -0.5

LANES = 128
VMEM_CAP_BYTES = 60000 * 1024

TM = 1040
TQ = 256
SUB_PAGES = 16
MEM_SAMPLES_PER_STEP = 8
MOE_TM = 512
ROW_TILE = 208
DMA_UNROLL = 8


def _nbytes(shape, dtype):
    n = 1
    for s in shape:
        n *= s
    return n * jnp.dtype(dtype).itemsize


def _params(sem, blocks, scratch=0, temps=0):
    need = 2 * sum(_nbytes(s, d) for s, d in blocks) + scratch + temps
    return pltpu.CompilerParams(dimension_semantics=sem, vmem_limit_bytes=min(need, VMEM_CAP_BYTES))


def _rms(x, g):
    return x * lax.rsqrt(jnp.mean(x * x, axis=-1, keepdims=True) + EPS) * g


def _gelu(x):
    return x * (0.5 * (1.0 + jnp.tanh(0.7978845608028654 * (x + 0.044715 * (x * x * x)))))


def _sigmoid(x):
    return 1.0 / (1.0 + jnp.exp(-x))


def _dot(a, b):
    return jnp.dot(a, b, preferred_element_type=F32)


def _dot_nt(a, b):
    return lax.dot_general(a, b, (((1,), (1,)), ((), ())), preferred_element_type=F32)


def _in_proj_kernel(x_ref, g_ref, w_ref, wkr_ref, cg_ref, cos_ref, sin_ref,
                    z_ref, kr_ref, kcat_ref, hn_ref):
    j = pl.program_id(1)

    @pl.when(j == 0)
    def _():
        hn = _rms(x_ref[...], g_ref[...]).astype(BF16)
        hn_ref[...] = hn
        kr2 = _dot(hn, wkr_ref[...])
        kr = kr2[:, :ROPE] * cos_ref[...] + kr2[:, ROPE:] * sin_ref[...]
        kr_ref[...] = kr
        kcat_ref[:, LORA:KCAT] = jnp.concatenate([kr, jnp.zeros_like(kr)], axis=-1).astype(BF16)

    acc = _dot(hn_ref[...], w_ref[...])
    cg = cg_ref[...]

    @pl.when(j <= 1)
    def _():
        y = _rms(acc, cg)
        z_ref[...] = y

        @pl.when(j == 1)
        def _():
            kcat_ref[:, 0:LORA] = y.astype(BF16)

    @pl.when((j == 2) | (j == 3))
    def _():
        z_ref[...] = _gelu(acc)

    @pl.when(j >= 4)
    def _():
        gl = _gelu(acc)
        for k in range(LORA // CHUNK):
            sl = slice(k * CHUNK, (k + 1) * CHUNK)
            z_ref[:, sl] = _rms(gl[:, sl], cg[:, sl])


def _in_proj(layer, x, g, w1, wkr, colgain, cos, sin):
    t, d = x.shape
    n = w1.shape[2]
    tn = LORA
    blocks = [((TM, d), F32), ((d, tn), BF16), ((d, 2 * ROPE), BF16), ((TM, tn), F32),
              ((TM, LANES), F32), ((TM, KCAT), BF16), ((TM, 2 * LANES), F32)]
    return pl.pallas_call(
        _in_proj_kernel,
        name="in_proj",
        grid=(t // TM, n // tn),
        in_specs=[
            pl.BlockSpec((TM, d), lambda i, j: (i, 0)),
            pl.BlockSpec((1, d), lambda i, j: (0, 0)),
            pl.BlockSpec((None, d, tn), lambda i, j: (layer, 0, j)),
            pl.BlockSpec((None, d, 2 * ROPE), lambda i, j: (layer, 0, 0)),
            pl.BlockSpec((None, 1, tn), lambda i, j: (layer, 0, j)),
            pl.BlockSpec((TM, ROPE), lambda i, j: (i, 0)),
            pl.BlockSpec((TM, ROPE), lambda i, j: (i, 0)),
        ],
        out_specs=[
            pl.BlockSpec((TM, tn), lambda i, j: (i, j)),
            pl.BlockSpec((TM, ROPE), lambda i, j: (i, 0)),
            pl.BlockSpec((TM, KCAT), lambda i, j: (i, 0)),
        ],
        out_shape=[
            jax.ShapeDtypeStruct((t, n), F32),
            jax.ShapeDtypeStruct((t, ROPE), F32),
            jax.ShapeDtypeStruct((t, KCAT), BF16),
        ],
        scratch_shapes=[pltpu.VMEM((TM, d), BF16)],
        compiler_params=_params(("parallel", "arbitrary"), blocks,
                                scratch=_nbytes((TM, d), BF16), temps=3 * _nbytes((TM, d), F32)),
    )(x, g, w1, wkr, colgain, cos, sin)


def _q_proj_kernel(cq_ref, wuq_ref, wuk_ref, cos_ref, sin_ref, q_ref):
    cq = cq_ref[...].astype(BF16)
    q = _dot(cq, wuq_ref[...])
    qa = _dot(q[:, :NOPE].astype(BF16), wuk_ref[...])
    qr = q[:, NOPE:NOPE + ROPE] * cos_ref[...] + q[:, NOPE + ROPE:] * sin_ref[...]
    q_ref[:, 0:LORA] = (qa * MLA_SCALE).astype(BF16)
    q_ref[:, LORA:KCAT] = jnp.concatenate([qr * MLA_SCALE, jnp.zeros_like(qr)], axis=-1).astype(BF16)


def _q_proj(layer, z, wuq2, wukt, cos, sin):
    t = z.shape[0]
    blocks = [((TM, LORA), F32), ((LORA, 2 * NOPE), BF16), ((NOPE, LORA), BF16),
              ((TM, LANES), F32), ((TM, LANES), F32), ((TM, KCAT), BF16)]
    return pl.pallas_call(
        _q_proj_kernel,
        name="q_proj",
        grid=(t // TM, HEADS),
        in_specs=[
            pl.BlockSpec((TM, LORA), lambda i, h: (i, 0)),
            pl.BlockSpec((None, None, LORA, 2 * NOPE), lambda i, h: (layer, h, 0, 0)),
            pl.BlockSpec((None, None, NOPE, LORA), lambda i, h: (layer, h, 0, 0)),
            pl.BlockSpec((TM, ROPE), lambda i, h: (i, 0)),
            pl.BlockSpec((TM, ROPE), lambda i, h: (i, 0)),
        ],
        out_specs=pl.BlockSpec((None, TM, KCAT), lambda i, h: (h, i, 0)),
        out_shape=jax.ShapeDtypeStruct((HEADS, t, KCAT), BF16),
        compiler_params=_params(("parallel", "arbitrary"), blocks, temps=4 * _nbytes((TM, LORA), F32)),
    )(z, wuq2, wukt, cos, sin)


def _attn_prompt_kernel(q_ref, k_ref, wuv_ref, o_ref, m_ref, l_ref, acc_ref, *, nq):
    step = pl.program_id(0)
    n_steps = pl.num_programs(0) - 1
    qi = step % nq
    rows = HEADS * TQ

    @pl.when(step < n_steps)
    def _():
        q = q_ref[...].reshape(rows, KCAT)
        m_ref[...] = jnp.full((rows, 1), -jnp.inf, F32)
        l_ref[...] = jnp.zeros((rows, 1), F32)
        acc_ref[...] = jnp.zeros((rows, LORA), F32)

        def chunk(c):
            return k_ref[pl.ds(pl.multiple_of(c * TQ, TQ), TQ), :]

        def update(s, kc):
            m_prev = m_ref[...]
            m_new = jnp.maximum(m_prev, jnp.max(s, axis=-1, keepdims=True))
            alpha = jnp.exp(m_prev - m_new)
            p = jnp.exp(s - m_new)
            l_ref[...] = alpha * l_ref[...] + jnp.sum(p, axis=-1, keepdims=True)
            acc_ref[...] = alpha * acc_ref[...] + _dot(p.astype(BF16), kc[:, :LORA])
            m_ref[...] = m_new

        def body(c, s):
            s_next = _dot_nt(q, chunk(c + 1))
            update(s, chunk(c))
            return s_next

        s = lax.fori_loop(0, qi, body, _dot_nt(q, chunk(0)))
        q_pos = lax.broadcasted_iota(I32, (rows, TQ), 0) & (TQ - 1)
        k_pos = lax.broadcasted_iota(I32, (rows, TQ), 1)
        update(jnp.where(k_pos <= q_pos, s, -jnp.inf), chunk(qi))
        o = acc_ref[...] * (1.0 / l_ref[...])
        for h in range(HEADS):
            oh = o[h * TQ:(h + 1) * TQ, :].astype(BF16)
            o_ref[:, h * VDIM:(h + 1) * VDIM] = _dot(oh, wuv_ref[h])

    @pl.when(step == n_steps)
    def _():
        o_ref[...] = jnp.zeros(o_ref.shape, F32)


def _attn_prompt(layer, q, kcat, wuv_h, batch, seq):
    t = kcat.shape[0]
    nq = seq // TQ
    n_steps = batch * nq
    rows = HEADS * TQ
    blocks = [((HEADS, TQ, KCAT), BF16), ((seq, KCAT), BF16), ((HEADS, LORA, VDIM), BF16),
              ((TQ, HEADS * VDIM), F32)]
    scratch = 2 * _nbytes((rows, LANES), F32) + _nbytes((rows, LORA), F32)
    return pl.pallas_call(
        functools.partial(_attn_prompt_kernel, nq=nq),
        name="attn_prompt",
        grid=(n_steps + 1,),
        in_specs=[
            pl.BlockSpec((HEADS, TQ, KCAT), lambda i: (0, jnp.minimum(i, n_steps - 1), 0)),
            pl.BlockSpec((seq, KCAT), lambda i: (jnp.minimum(i, n_steps - 1) // nq, 0)),
            pl.BlockSpec((None, HEADS, LORA, VDIM), lambda i: (layer, 0, 0, 0)),
        ],
        out_specs=pl.BlockSpec((TQ, HEADS * VDIM), lambda i: (i, 0)),
        out_shape=jax.ShapeDtypeStruct((t, HEADS * VDIM), F32),
        scratch_shapes=[pltpu.VMEM((rows, 1), F32), pltpu.VMEM((rows, 1), F32),
                        pltpu.VMEM((rows, LORA), F32)],
        compiler_params=_params(("arbitrary",), blocks, scratch=scratch,
                                temps=6 * _nbytes((rows, TQ), F32) + 2 * _nbytes((rows, LORA), F32)),
    )(q, kcat, wuv_h)


def _attn_sample_kernel(pt_ref, q_ref, ks_ref, wuv_ref, lat_hbm, krt_hbm, oin_ref, o_ref,
                        latbuf_ref, krtbuf_ref, sem, *, layer, n_pages):
    del oin_ref
    b = pl.program_id(0)
    slot = b % 2
    page = lat_hbm.shape[2]
    keys = SUB_PAGES * page

    def page_copies(sample, dst_slot):
        copies = []
        for g in range(n_pages):
            pg = pt_ref[sample * n_pages + g]
            span = pl.ds(g * page, page)
            copies.append(pltpu.make_async_copy(lat_hbm.at[layer, pg], latbuf_ref.at[dst_slot, span, :],
                                                sem.at[0, dst_slot]))
            copies.append(pltpu.make_async_copy(krt_hbm.at[layer, pg], krtbuf_ref.at[dst_slot, :, span],
                                                sem.at[1, dst_slot]))
        return copies

    @pl.when(b == 0)
    def _():
        for c in page_copies(b, slot):
            c.start()

    @pl.when(b + 1 < pl.num_programs(0))
    def _():
        for c in page_copies(b + 1, 1 - slot):
            c.start()

    for c in page_copies(b, slot):
        c.wait()

    q = q_ref[0]
    qa = jnp.concatenate([q[:, :LORA], jnp.zeros((LANES - HEADS, LORA), BF16)], axis=0)
    qr = q[:, LORA:LORA + ROPE]
    n_sub = n_pages // SUB_PAGES
    lats = [latbuf_ref[slot, j * keys:(j + 1) * keys, :].astype(BF16) for j in range(n_sub)]
    s_ts = [_dot_nt(lats[j], qa) for j in range(n_sub)]
    parts = []
    for j in range(n_sub):
        krt = krtbuf_ref[slot, :, j * keys:(j + 1) * keys].astype(BF16)
        s = s_ts[j].T[:HEADS, :] + _dot(qr, krt)
        m_j = jnp.max(s, axis=-1, keepdims=True)
        p = jnp.exp(s - m_j)
        parts.append((m_j, jnp.sum(p, axis=-1, keepdims=True), _dot(p.astype(BF16), lats[j])))

    q = q.astype(F32)

    k_self = ks_ref[0].astype(F32)
    s_self = jnp.sum(q * k_self, axis=-1, keepdims=True)
    m = s_self
    for m_j, _, _ in parts:
        m = jnp.maximum(m, m_j)
    w_self = jnp.exp(s_self - m)
    l = w_self
    acc = w_self * k_self[:, :LORA]
    for m_j, l_j, acc_j in parts:
        a = jnp.exp(m_j - m)
        l = l + a * l_j
        acc = acc + a * acc_j
    o_lat = acc * (1.0 / l)
    r = _dot(o_lat.astype(BF16), wuv_ref[...])
    head = lax.broadcasted_iota(I32, r.shape, 0)
    col_head = lax.broadcasted_iota(I32, r.shape, 1) // VDIM
    o_ref[pl.ds(b, 1), :] = jnp.sum(jnp.where(head == col_head, r, 0.0), axis=0, keepdims=True)


def _attn_sample(layer, page_table, qs, ks, wuv_all, o_att, cache_lat, cache_krt):
    nb, n_pages = page_table.shape
    page = cache_lat.shape[2]
    past = n_pages * page
    t = o_att.shape[0]
    row_block = (t - nb) // nb
    pt_flat = page_table.reshape(-1)
    keys = SUB_PAGES * page
    blocks = [((HEADS, KCAT), BF16), ((1, KCAT), BF16), ((LORA, HEADS * VDIM), BF16),
              ((nb, HEADS * VDIM), F32)]
    scratch = _nbytes((2, past, LORA), F32) + _nbytes((2, ROPE, past), F32)
    grid_spec = pltpu.PrefetchScalarGridSpec(
        num_scalar_prefetch=1,
        grid=(nb,),
        in_specs=[
            pl.BlockSpec((1, HEADS, KCAT), lambda b, pt: (b, 0, 0)),
            pl.BlockSpec((1, 1, KCAT), lambda b, pt: (b, 0, 0)),
            pl.BlockSpec((None, LORA, HEADS * VDIM), lambda b, pt: (layer, 0, 0)),
            pl.BlockSpec(memory_space=pl.ANY),
            pl.BlockSpec(memory_space=pl.ANY),
            pl.BlockSpec(memory_space=pl.ANY),
        ],
        out_specs=pl.BlockSpec((nb, HEADS * VDIM), lambda b, pt: (row_block, 0)),
        scratch_shapes=[pltpu.VMEM((2, past, LORA), F32), pltpu.VMEM((2, ROPE, past), F32),
                        pltpu.SemaphoreType.DMA((2, 2))],
    )
    return pl.pallas_call(
        functools.partial(_attn_sample_kernel, layer=layer, n_pages=n_pages),
        name="attn_sample",
        grid_spec=grid_spec,
        out_shape=jax.ShapeDtypeStruct(o_att.shape, F32),
        input_output_aliases={6: 0},
        compiler_params=_params(("arbitrary",), blocks, scratch=scratch,
                                temps=(n_pages // SUB_PAGES + 2) * (_nbytes((keys, LANES), F32)
                                                                    + _nbytes((keys, LORA), BF16))),
    )(pt_flat, qs, ks, wuv_all, cache_lat, cache_krt, o_att)


def _chunk_prompt_kernel(u_ref, v_ref, ws_ref, bst_ref, o_ref):
    step = pl.program_id(0)
    n_steps = pl.num_programs(0) - 1

    @pl.when(step < n_steps)
    def _():
        row = lax.broadcasted_iota(I32, (CHUNK, CHUNK), 0)
        col = lax.broadcasted_iota(I32, (CHUNK, CHUNK), 1)
        n_chunks = u_ref.shape[0] // CHUNK
        for g in range(GROUPS):
            w = jnp.where(row >= col, ws_ref[g], 0.0).astype(BF16)
            bias = bst_ref[:, g:g + 1]
            cs = slice(g * CHUNK, (g + 1) * CHUNK)
            for n in range(n_chunks):
                rs = slice(n * CHUNK, (n + 1) * CHUNK)
                mixed = _dot(w, v_ref[rs, cs].astype(BF16)) + bias
                o_ref[rs, cs] = u_ref[rs, cs] * mixed

    @pl.when(step == n_steps)
    def _():
        o_ref[...] = jnp.zeros(o_ref.shape, F32)


def _chunk_prompt(layer, z, ws, bst, n_prompt):
    t = z.shape[0]
    width = GROUPS * CHUNK
    rows = 4 * CHUNK
    n_steps = n_prompt // rows
    blocks = [((rows, width), F32)] * 3 + [((GROUPS, CHUNK, CHUNK), F32), ((CHUNK, LANES), F32)]
    return pl.pallas_call(
        _chunk_prompt_kernel,
        name="chunk_prompt",
        grid=(n_steps + 1,),
        in_specs=[
            pl.BlockSpec((rows, width), lambda i: (jnp.minimum(i, n_steps - 1), 1)),
            pl.BlockSpec((rows, width), lambda i: (jnp.minimum(i, n_steps - 1), 2)),
            pl.BlockSpec((None, GROUPS, CHUNK, CHUNK), lambda i: (layer, 0, 0, 0)),
            pl.BlockSpec((None, CHUNK, GROUPS), lambda i: (layer, 0, 0)),
        ],
        out_specs=pl.BlockSpec((rows, width), lambda i: (i, 0)),
        out_shape=jax.ShapeDtypeStruct((t, width), F32),
        compiler_params=_params(("arbitrary",), blocks, temps=_nbytes((rows, width), F32)),
    )(z, z, ws, bst)


def _chunk_sample_kernel(u_ref, v_ref, a_ref, c_ref, oin_ref, o_ref):
    del oin_ref
    o_ref[...] = u_ref[...] * (a_ref[...] * v_ref[...] + c_ref[...])


def _chunk_sample(z, a, c, o_chk, n_sample):
    t = z.shape[0]
    width = GROUPS * CHUNK
    rb = (t - n_sample) // n_sample
    blocks = [((n_sample, width), F32)] * 3 + [((8, width), F32)] * 2
    return pl.pallas_call(
        _chunk_sample_kernel,
        name="chunk_sample",
        grid=(1,),
        in_specs=[
            pl.BlockSpec((n_sample, width), lambda i: (rb, 1)),
            pl.BlockSpec((n_sample, width), lambda i: (rb, 2)),
            pl.BlockSpec((1, width), lambda i: (0, 0)),
            pl.BlockSpec((1, width), lambda i: (0, 0)),
            pl.BlockSpec(memory_space=pl.ANY),
        ],
        out_specs=pl.BlockSpec((n_sample, width), lambda i: (rb, 0)),
        out_shape=jax.ShapeDtypeStruct(o_chk.shape, F32),
        input_output_aliases={4: 0},
        compiler_params=_params(("arbitrary",), blocks),
    )(z, z, a, c, o_chk)


def _out_proj_kernel(oa_ref, oc_ref, ga_ref, gc_ref, w_ref, x_ref, o_ref, cat_ref):
    half = oa_ref.shape[1]

    @pl.when(pl.program_id(1) == 0)
    def _():
        cat_ref[:, 0:half] = _rms(oa_ref[...], ga_ref[...]).astype(BF16)
        cat_ref[:, half:] = _rms(oc_ref[...], gc_ref[...]).astype(BF16)

    o_ref[...] = x_ref[...] + _dot(cat_ref[...], w_ref[...])


def _out_proj(layer, o_att, o_chk, ga, gc, w, x):
    t, d = x.shape
    half = o_att.shape[1]
    tn = 512
    blocks = [((TM, half), F32)] * 2 + [((d, tn), BF16), ((TM, tn), F32), ((TM, tn), F32)]
    return pl.pallas_call(
        _out_proj_kernel,
        name="out_proj",
        grid=(t // TM, d // tn),
        in_specs=[
            pl.BlockSpec((TM, half), lambda i, j: (i, 0)),
            pl.BlockSpec((TM, half), lambda i, j: (i, 0)),
            pl.BlockSpec((1, half), lambda i, j: (0, 0)),
            pl.BlockSpec((1, half), lambda i, j: (0, 0)),
            pl.BlockSpec((None, d, tn), lambda i, j: (layer, 0, j)),
            pl.BlockSpec((TM, tn), lambda i, j: (i, j)),
        ],
        out_specs=pl.BlockSpec((TM, tn), lambda i, j: (i, j)),
        out_shape=jax.ShapeDtypeStruct((t, d), F32),
        scratch_shapes=[pltpu.VMEM((TM, d), BF16)],
        compiler_params=_params(("parallel", "arbitrary"), blocks, scratch=_nbytes((TM, d), BF16),
                                temps=3 * _nbytes((TM, half), F32)),
    )(o_att, o_chk, ga, gc, w, x)


def _norm_mm_kernel(x_ref, g_ref, w_ref, o_ref, hn_ref):
    @pl.when(pl.program_id(1) == 0)
    def _():
        hn_ref[...] = _rms(x_ref[...], g_ref[...]).astype(BF16)

    o_ref[...] = _dot(hn_ref[...], w_ref[...])


def _norm_mm(layer, x, g, w, tm, tn):
    m, d = x.shape
    n = w.shape[2]
    blocks = [((tm, d), F32), ((d, tn), BF16), ((tm, tn), F32)]
    return pl.pallas_call(
        _norm_mm_kernel,
        name="mem_kv",
        grid=(m // tm, n // tn),
        in_specs=[
            pl.BlockSpec((tm, d), lambda i, j: (i, 0)),
            pl.BlockSpec((1, d), lambda i, j: (0, 0)),
            pl.BlockSpec((None, d, tn), lambda i, j: (layer, 0, j)),
        ],
        out_specs=pl.BlockSpec((tm, tn), lambda i, j: (i, j)),
        out_shape=jax.ShapeDtypeStruct((m, n), F32),
        scratch_shapes=[pltpu.VMEM((tm, d), BF16)],
        compiler_params=_params(("parallel", "arbitrary"), blocks, scratch=_nbytes((tm, d), BF16),
                                temps=3 * _nbytes((tm, d), F32)),
    )(x, g, w)


def _mem_prompt_kernel(x_ref, g_ref, wq_ref, kv_ref, wo_ref, o_ref):
    step = pl.program_id(0)
    n_steps = pl.num_programs(0) - 1

    @pl.when(step < n_steps)
    def _():
        x = x_ref[...]
        q = _dot(_rms(x, g_ref[...]).astype(BF16), wq_ref[...])
        width = MEM_HEADS * MEM_DIM
        outs = []
        for h in range(MEM_HEADS):
            hs = slice(h * MEM_DIM, (h + 1) * MEM_DIM)
            vs = slice(width + h * MEM_DIM, width + (h + 1) * MEM_DIM)
            s = _dot_nt(q[:, hs].astype(BF16), kv_ref[:, hs].astype(BF16)) * MEM_SCALE
            p = jnp.exp(s - jnp.max(s, axis=-1, keepdims=True))
            p = p * (1.0 / jnp.sum(p, axis=-1, keepdims=True))
            outs.append(_dot(p.astype(BF16), kv_ref[:, vs].astype(BF16)))
        o = jnp.concatenate(outs, axis=-1).astype(BF16)
        o_ref[...] = x + _dot(o, wo_ref[...])

    @pl.when(step == n_steps)
    def _():
        o_ref[...] = jnp.zeros(o_ref.shape, F32)


def _mem_prompt(layer, x, g, wq, mkv, wo, n_prompt, seq, mem_tokens):
    t, d = x.shape
    rows = 512
    width = MEM_HEADS * MEM_DIM
    per_batch = seq // rows
    n_steps = n_prompt // rows
    blocks = [((rows, d), F32), ((d, width), BF16), ((mem_tokens, 2 * width), F32),
              ((width, d), BF16), ((rows, d), F32)]
    return pl.pallas_call(
        _mem_prompt_kernel,
        name="mem_prompt",
        grid=(n_steps + 1,),
        in_specs=[
            pl.BlockSpec((rows, d), lambda i: (jnp.minimum(i, n_steps - 1), 0)),
            pl.BlockSpec((1, d), lambda i: (0, 0)),
            pl.BlockSpec((None, d, width), lambda i: (layer, 0, 0)),
            pl.BlockSpec((mem_tokens, 2 * width), lambda i: (jnp.minimum(i, n_steps - 1) // per_batch, 0)),
            pl.BlockSpec((None, width, d), lambda i: (layer, 0, 0)),
        ],
        out_specs=pl.BlockSpec((rows, d), lambda i: (i, 0)),
        out_shape=jax.ShapeDtypeStruct((t, d), F32),
        compiler_params=_params(("arbitrary",), blocks, temps=4 * _nbytes((rows, d), F32)),
    )(x, g, wq, mkv, wo)


def _mem_sample_kernel(x_ref, g_ref, wq_ref, k_ref, v_ref, wo_ref, xin_ref, o_ref, q_scr, o_scr):
    del xin_ref
    step = pl.program_id(0)
    width = MEM_HEADS * MEM_DIM

    @pl.when(step == 0)
    def _():
        q_scr[...] = _dot(_rms(x_ref[...], g_ref[...]).astype(BF16), wq_ref[...])

    kv_rows = k_ref.shape[1]
    row_head = lax.broadcasted_iota(I32, (8, kv_rows), 1) & (MEM_HEADS - 1)
    sub = lax.broadcasted_iota(I32, (8, kv_rows), 0)
    own = (row_head == sub) | (sub >= MEM_HEADS)
    pad = jnp.zeros((8 - MEM_HEADS, MEM_DIM), F32)
    for n in range(MEM_SAMPLES_PER_STEP):
        r = step * MEM_SAMPLES_PER_STEP + n
        q_row = q_scr[pl.ds(r, 1), :]
        q_heads = jnp.concatenate([q_row[:, h * MEM_DIM:(h + 1) * MEM_DIM] for h in range(MEM_HEADS)]
                                  + [pad], axis=0)
        s = _dot_nt(q_heads.astype(BF16), k_ref[n].astype(BF16)) * MEM_SCALE
        s = jnp.where(own, s, -jnp.inf)
        p = jnp.exp(s - jnp.max(s, axis=-1, keepdims=True))
        p = p * (1.0 / jnp.sum(p, axis=-1, keepdims=True))
        ob = _dot(p.astype(BF16), v_ref[n].astype(BF16))
        o_scr[pl.ds(r, 1), :] = jnp.concatenate([ob[h:h + 1, :] for h in range(MEM_HEADS)], axis=1)

    @pl.when(step == pl.num_programs(0) - 1)
    def _():
        o_ref[...] = x_ref[...] + _dot(o_scr[...].astype(BF16), wo_ref[...])


def _mem_sample(layer, x_mid, x_new, g, wq, wo, cache_k, cache_v, n_sample):
    t, d = x_mid.shape
    width = MEM_HEADS * MEM_DIM
    kv_rows = cache_k.shape[2]
    per = MEM_SAMPLES_PER_STEP
    rb = (t - n_sample) // n_sample
    blocks = [((n_sample, d), F32), ((d, width), BF16), ((per, kv_rows, MEM_DIM), F32),
              ((per, kv_rows, MEM_DIM), F32), ((width, d), BF16), ((n_sample, d), F32)]
    return pl.pallas_call(
        _mem_sample_kernel,
        name="mem_sample",
        grid=(n_sample // per,),
        in_specs=[
            pl.BlockSpec((n_sample, d), lambda i: (rb, 0)),
            pl.BlockSpec((1, d), lambda i: (0, 0)),
            pl.BlockSpec((None, d, width), lambda i: (layer, 0, 0)),
            pl.BlockSpec((None, per, kv_rows, MEM_DIM), lambda i: (layer, i, 0, 0)),
            pl.BlockSpec((None, per, kv_rows, MEM_DIM), lambda i: (layer, i, 0, 0)),
            pl.BlockSpec((None, width, d), lambda i: (layer, 0, 0)),
            pl.BlockSpec(memory_space=pl.ANY),
        ],
        out_specs=pl.BlockSpec((n_sample, d), lambda i: (rb, 0)),
        out_shape=jax.ShapeDtypeStruct((t, d), F32),
        scratch_shapes=[pltpu.VMEM((n_sample, width), F32), pltpu.VMEM((n_sample, width), F32)],
        input_output_aliases={6: 0},
        compiler_params=_params(("arbitrary",), blocks, scratch=2 * _nbytes((n_sample, width), F32),
                                temps=4 * _nbytes((n_sample, d), F32)),
    )(x_mid, g, wq, cache_k, cache_v, wo, x_new)


def _ffn_up_kernel(x_ref, g_ref, wg_ref, wu_ref, o_ref, hn_ref):
    @pl.when(pl.program_id(1) == 0)
    def _():
        hn_ref[...] = _rms(x_ref[...], g_ref[...]).astype(BF16)

    hn = hn_ref[...]
    a = _dot(hn, wg_ref[...])
    b = _dot(hn, wu_ref[...])
    o_ref[...] = (a * _sigmoid(a) * b).astype(BF16)


def _ffn_up(layer, x, g, wg, wu):
    t, d = x.shape
    f = wg.shape[2]
    tn = 512
    blocks = [((TM, d), F32), ((d, tn), BF16), ((d, tn), BF16), ((TM, tn), BF16)]
    return pl.pallas_call(
        _ffn_up_kernel,
        name="ffn_up",
        grid=(t // TM, f // tn),
        in_specs=[
            pl.BlockSpec((TM, d), lambda i, j: (i, 0)),
            pl.BlockSpec((1, d), lambda i, j: (0, 0)),
            pl.BlockSpec((None, d, tn), lambda i, j: (layer, 0, j)),
            pl.BlockSpec((None, d, tn), lambda i, j: (layer, 0, j)),
        ],
        out_specs=pl.BlockSpec((TM, tn), lambda i, j: (i, j)),
        out_shape=jax.ShapeDtypeStruct((t, f), BF16),
        scratch_shapes=[pltpu.VMEM((TM, d), BF16)],
        compiler_params=_params(("parallel", "arbitrary"), blocks, scratch=_nbytes((TM, d), BF16),
                                temps=3 * _nbytes((TM, d), F32)),
    )(x, g, wg, wu)


def _mm_res_kernel(a_ref, w_ref, x_ref, o_ref):
    o_ref[...] = x_ref[...] + _dot(a_ref[...], w_ref[...])


def _ffn_down(layer, act, wd, x):
    t, f = act.shape
    d = wd.shape[2]
    tm, tn = 640, 512
    blocks = [((tm, f), BF16), ((f, tn), BF16), ((tm, tn), F32), ((tm, tn), F32)]
    return pl.pallas_call(
        _mm_res_kernel,
        name="ffn_down",
        grid=(t // tm, d // tn),
        in_specs=[
            pl.BlockSpec((tm, f), lambda i, j: (i, 0)),
            pl.BlockSpec((None, f, tn), lambda i, j: (layer, 0, j)),
            pl.BlockSpec((tm, tn), lambda i, j: (i, j)),
        ],
        out_specs=pl.BlockSpec((tm, tn), lambda i, j: (i, j)),
        out_shape=jax.ShapeDtypeStruct((t, d), F32),
        compiler_params=_params(("parallel", "arbitrary"), blocks, temps=2 * _nbytes((tm, tn), F32)),
    )(act, wd, x)


def _router_kernel(x_ref, g_ref, whi_ref, wlo_ref, idx_ref, wgt_ref):
    h = _rms(x_ref[...], g_ref[...])
    h_hi = h.astype(BF16)
    h_lo = (h - h_hi.astype(F32)).astype(BF16)
    logits = _dot(h_hi, whi_ref[...]) + (_dot(h_lo, whi_ref[...]) + _dot(h_hi, wlo_ref[...]))
    lane = lax.broadcasted_iota(I32, logits.shape, 1)
    logits = jnp.where(lane < N_EXPERTS, logits, -jnp.inf)
    v1 = jnp.max(logits, axis=-1, keepdims=True)
    i1 = jnp.min(jnp.where(logits == v1, lane, LANES), axis=-1, keepdims=True)
    rest = jnp.where(lane == i1, -jnp.inf, logits)
    v2 = jnp.max(rest, axis=-1, keepdims=True)
    i2 = jnp.min(jnp.where(rest == v2, lane, LANES), axis=-1, keepdims=True)
    e2 = jnp.exp(v2 - v1)
    w1 = 1.0 / (1.0 + e2)
    w2 = e2 * w1
    idx_ref[...] = jnp.where(lane == 0, i1, jnp.where(lane == 1, i2, 0))
    wgt_ref[...] = jnp.where(lane == 0, w1, jnp.where(lane == 1, w2, 0.0))


def _router(x, g, whi, wlo):
    t, d = x.shape
    blocks = [((TM, d), F32), ((d, LANES), BF16), ((d, LANES), BF16), ((TM, LANES), I32), ((TM, LANES), F32)]
    return pl.pallas_call(
        _router_kernel,
        name="router",
        grid=(t // TM,),
        in_specs=[
            pl.BlockSpec((TM, d), lambda i: (i, 0)),
            pl.BlockSpec((1, d), lambda i: (0, 0)),
            pl.BlockSpec((d, LANES), lambda i: (0, 0)),
            pl.BlockSpec((d, LANES), lambda i: (0, 0)),
        ],
        out_specs=[pl.BlockSpec((TM, LANES), lambda i: (i, 0)), pl.BlockSpec((TM, LANES), lambda i: (i, 0))],
        out_shape=[jax.ShapeDtypeStruct((t, LANES), I32), jax.ShapeDtypeStruct((t, LANES), F32)],
        compiler_params=_params(("parallel",), blocks, temps=5 * _nbytes((TM, d), F32)),
    )(x, g, whi, wlo)


def _dispatch_kernel(pos_ref, x_ref, g_ref, xs_in_ref, xs_ref, buf_ref, sem):
    del xs_in_ref
    i = pl.program_id(0)
    rows, d = x_ref.shape
    half = d // 2
    slot = i % 2
    h = _rms(x_ref[...], g_ref[...]).astype(BF16).astype(F32)
    lo = pltpu.bitcast(h[:, :half], U32) >> jnp.uint32(16)
    hi = pltpu.bitcast(h[:, half:], U32)
    buf_ref[slot] = lo | hi

    def row_copy(step, sl, r, k):
        dst = pos_ref[2 * (step * rows + r) + k]
        return pltpu.make_async_copy(buf_ref.at[sl, pl.ds(r, 1), :], xs_ref.at[pl.ds(dst, 1), :], sem.at[sl])

    def start_rows(step, sl):
        def body(r, carry):
            row_copy(step, sl, r, 0).start()
            row_copy(step, sl, r, 1).start()
            return carry
        lax.fori_loop(0, rows, body, 0, unroll=DMA_UNROLL)

    def wait_rows(step, sl):
        def body(r, carry):
            row_copy(step, sl, r, 0).wait()
            row_copy(step, sl, r, 1).wait()
            return carry
        lax.fori_loop(0, rows, body, 0, unroll=DMA_UNROLL)

    start_rows(i, slot)

    @pl.when(i > 0)
    def _():
        wait_rows(i - 1, 1 - slot)

    @pl.when(i == pl.num_programs(0) - 1)
    def _():
        wait_rows(i, slot)


def _dispatch(pos, x, g, xs_init):
    t, d = x.shape
    blocks = [((ROW_TILE, d), F32)]
    grid_spec = pltpu.PrefetchScalarGridSpec(
        num_scalar_prefetch=1,
        grid=(t // ROW_TILE,),
        in_specs=[
            pl.BlockSpec((ROW_TILE, d), lambda i, pos: (i, 0)),
            pl.BlockSpec((1, d), lambda i, pos: (0, 0)),
            pl.BlockSpec(memory_space=pl.ANY),
        ],
        out_specs=pl.BlockSpec(memory_space=pl.ANY),
        scratch_shapes=[pltpu.VMEM((2, ROW_TILE, d // 2), U32), pltpu.SemaphoreType.DMA((2,))],
    )
    return pl.pallas_call(
        _dispatch_kernel,
        name="moe_dispatch",
        grid_spec=grid_spec,
        out_shape=jax.ShapeDtypeStruct(xs_init.shape, U32),
        input_output_aliases={3: 0},
        compiler_params=_params(("arbitrary",), blocks, scratch=_nbytes((2, ROW_TILE, d // 2), U32),
                                temps=4 * _nbytes((ROW_TILE, d), F32)),
    )(pos, x, g, xs_init)


def _unpack_pairs(u):
    lo = pltpu.bitcast(u << jnp.uint32(16), F32).astype(BF16)
    hi = pltpu.bitcast(u & jnp.uint32(0xFFFF0000), F32).astype(BF16)
    return lo, hi


def _moe_up_kernel(te_ref, nt_ref, x_ref, wg_ref, wu_ref, o_ref):
    del te_ref
    i = pl.program_id(1)
    half = x_ref.shape[1]

    @pl.when(i < nt_ref[0])
    def _():
        lo, hi = _unpack_pairs(x_ref[...])
        a = _dot(lo, wg_ref[0:half, :]) + _dot(hi, wg_ref[half:, :])
        b = _dot(lo, wu_ref[0:half, :]) + _dot(hi, wu_ref[half:, :])
        o_ref[...] = (a * _sigmoid(a) * b).astype(BF16)

    @pl.when(i >= nt_ref[0])
    def _():
        o_ref[...] = jnp.zeros(o_ref.shape, BF16)


def _moe_up(layer, tile_expert, n_tiles, xs, wg, wu):
    r, half = xs.shape
    d, f = wg.shape[2], wg.shape[3]
    tf = f // 2
    blocks = [((MOE_TM, half), U32), ((d, tf), BF16), ((d, tf), BF16), ((MOE_TM, tf), BF16)]
    grid_spec = pltpu.PrefetchScalarGridSpec(
        num_scalar_prefetch=2,
        grid=(f // tf, r // MOE_TM),
        in_specs=[
            pl.BlockSpec((MOE_TM, half), lambda j, i, te, nt: (i, 0)),
            pl.BlockSpec((None, None, d, tf), lambda j, i, te, nt: (layer, te[i], 0, j)),
            pl.BlockSpec((None, None, d, tf), lambda j, i, te, nt: (layer, te[i], 0, j)),
        ],
        out_specs=pl.BlockSpec((MOE_TM, tf), lambda j, i, te, nt: (i, j)),
    )
    return pl.pallas_call(
        _moe_up_kernel,
        name="moe_up",
        grid_spec=grid_spec,
        out_shape=jax.ShapeDtypeStruct((r, f), BF16),
        compiler_params=_params(("arbitrary", "arbitrary"), blocks, temps=6 * _nbytes((MOE_TM, tf), F32)),
    )(tile_expert, n_tiles, xs, wg, wu)


def _moe_down_kernel(te_ref, nt_ref, a_ref, w_ref, o_ref):
    del te_ref
    i = pl.program_id(1)

    @pl.when(i < nt_ref[0])
    def _():
        o_ref[...] = _dot(a_ref[...], w_ref[...])

    @pl.when(i >= nt_ref[0])
    def _():
        o_ref[...] = jnp.zeros(o_ref.shape, F32)


def _moe_down(layer, tile_expert, n_tiles, act, wd):
    r, f = act.shape
    d = wd.shape[3]
    tn = d // 2
    blocks = [((MOE_TM, f), BF16), ((f, tn), BF16), ((MOE_TM, tn), F32)]
    grid_spec = pltpu.PrefetchScalarGridSpec(
        num_scalar_prefetch=2,
        grid=(d // tn, r // MOE_TM),
        in_specs=[
            pl.BlockSpec((MOE_TM, f), lambda j, i, te, nt: (i, 0)),
            pl.BlockSpec((None, None, f, tn), lambda j, i, te, nt: (layer, te[i], 0, j)),
        ],
        out_specs=pl.BlockSpec((MOE_TM, tn), lambda j, i, te, nt: (i, j)),
    )
    return pl.pallas_call(
        _moe_down_kernel,
        name="moe_down",
        grid_spec=grid_spec,
        out_shape=jax.ShapeDtypeStruct((r, d), F32),
        compiler_params=_params(("arbitrary", "arbitrary"), blocks, temps=2 * _nbytes((MOE_TM, tn), F32)),
    )(tile_expert, n_tiles, act, wd)


def _combine_kernel(pos_ref, x_ref, wgt_ref, y_ref, o_ref, buf_ref, sem):
    i = pl.program_id(0)
    rows = x_ref.shape[0]
    slot = i % 2

    def row_copy(step, sl, r, k):
        src = pos_ref[2 * (step * rows + r) + k]
        return pltpu.make_async_copy(y_ref.at[pl.ds(src, 1), :], buf_ref.at[sl, k, pl.ds(r, 1), :], sem.at[sl])

    def start_rows(step, sl):
        def body(r, carry):
            row_copy(step, sl, r, 0).start()
            row_copy(step, sl, r, 1).start()
            return carry
        lax.fori_loop(0, rows, body, 0, unroll=DMA_UNROLL)

    def wait_rows(step, sl):
        def body(r, carry):
            row_copy(step, sl, r, 0).wait()
            row_copy(step, sl, r, 1).wait()
            return carry
        lax.fori_loop(0, rows, body, 0, unroll=DMA_UNROLL)

    @pl.when(i == 0)
    def _():
        start_rows(i, slot)

    @pl.when(i + 1 < pl.num_programs(0))
    def _():
        start_rows(i + 1, 1 - slot)

    wait_rows(i, slot)
    w = wgt_ref[...]
    o_ref[...] = x_ref[...] + (w[:, 0:1] * buf_ref[slot, 0] + w[:, 1:2] * buf_ref[slot, 1])


def _combine(pos, x, wgt, y):
    t, d = x.shape
    blocks = [((ROW_TILE, d), F32), ((ROW_TILE, LANES), F32), ((ROW_TILE, d), F32)]
    grid_spec = pltpu.PrefetchScalarGridSpec(
        num_scalar_prefetch=1,
        grid=(t // ROW_TILE,),
        in_specs=[
            pl.BlockSpec((ROW_TILE, d), lambda i, pos: (i, 0)),
            pl.BlockSpec((ROW_TILE, LANES), lambda i, pos: (i, 0)),
            pl.BlockSpec(memory_space=pl.ANY),
        ],
        out_specs=pl.BlockSpec((ROW_TILE, d), lambda i, pos: (i, 0)),
        scratch_shapes=[pltpu.VMEM((2, 2, ROW_TILE, d), F32), pltpu.SemaphoreType.DMA((2,))],
    )
    return pl.pallas_call(
        _combine_kernel,
        name="moe_combine",
        grid_spec=grid_spec,
        out_shape=jax.ShapeDtypeStruct((t, d), F32),
        compiler_params=_params(("arbitrary",), blocks, scratch=_nbytes((2, 2, ROW_TILE, d), F32),
                                temps=2 * _nbytes((ROW_TILE, d), F32)),
    )(pos, x, wgt, y)


def _route_plan(idx, n_rows_pad):
    flat_e = idx.reshape(-1)
    onehot = (flat_e[:, None] == jnp.arange(N_EXPERTS, dtype=I32)[None, :]).astype(I32)
    csum = jnp.cumsum(onehot, axis=0)
    counts = csum[-1]
    rank = jnp.sum(onehot * (csum - 1), axis=1)
    tiles_per = (counts + MOE_TM - 1) // MOE_TM
    ends = jnp.cumsum(tiles_per)
    row_start = (ends - tiles_per) * MOE_TM
    pos = jnp.sum(onehot * row_start[None, :], axis=1) + rank
    n_tiles = ends[-1]
    tile_ids = jnp.arange(n_rows_pad // MOE_TM, dtype=I32)
    tile_expert = jnp.sum((tile_ids[:, None] >= ends[None, :]).astype(I32), axis=1)
    last_expert = jnp.sum((n_tiles - 1 >= ends).astype(I32))
    tile_expert = jnp.minimum(tile_expert, last_expert)
    return pos.astype(I32), tile_expert.astype(I32), n_tiles.reshape(1).astype(I32)


def _moe(layer, x, g, whi, wlo, wg, wu, wd):
    t, d = x.shape
    idx, wgt = _router(x, g, whi, wlo)
    n_pairs = 2 * t
    n_rows_pad = ((n_pairs + N_EXPERTS * (MOE_TM - 1)) // MOE_TM) * MOE_TM
    pos, tile_expert, n_tiles = _route_plan(idx[:, :2], n_rows_pad)
    xs = _dispatch(pos, x, g, jnp.zeros((n_rows_pad, d // 2), U32))
    act = _moe_up(layer, tile_expert, n_tiles, xs, wg, wu)
    y = _moe_down(layer, tile_expert, n_tiles, act, wd)
    return _combine(pos, x, wgt, y)


def _final_norm_kernel(x_ref, g_ref, o_ref):
    o_ref[...] = _rms(x_ref[...], g_ref[...])


def _final_norm(x, g):
    t, d = x.shape
    blocks = [((TM, d), F32), ((TM, d), F32)]
    return pl.pallas_call(
        _final_norm_kernel,
        name="final_norm",
        grid=(t // TM,),
        in_specs=[pl.BlockSpec((TM, d), lambda i: (i, 0)), pl.BlockSpec((1, d), lambda i: (0, 0))],
        out_specs=pl.BlockSpec((TM, d), lambda i: (i, 0)),
        out_shape=jax.ShapeDtypeStruct((t, d), F32),
        compiler_params=_params(("parallel",), blocks, temps=2 * _nbytes((TM, d), F32)),
    )(x, g)


def _rotate_half_cols(w):
    half = w.shape[-1] // 2
    return jnp.concatenate([-w[..., half:], w[..., :half]], axis=-1)


def kernel(x_prompt, x_sample, mem_prompt, cache_kv_latent, cache_k_rope, cache_mem_k, cache_mem_v, page_table, norm_mix, w_in, q_norm, w_uq, kv_norm, w_uk, w_uv, chunk_v_norm, w_spatial, b_spatial, out_norm_mla, out_norm_chunk, w_out, norm_mem_q, norm_mem_kv, w_mem_q, w_mem_k, w_mem_v, w_mem_o, norm_ffn, w_gate_dense, w_up_dense, w_down_dense, w_router, w_gate_moe, w_up_moe, w_down_moe, final_norm):
    batch, seq, d = x_prompt.shape
    n_sample = x_sample.shape[0]
    depth = w_in.shape[0]
    mem_tokens = mem_prompt.shape[1]
    n_prompt = batch * seq
    t = n_prompt + n_sample
    past_len = page_table.shape[1] * cache_kv_latent.shape[2]
    mem_width = MEM_HEADS * MEM_DIM
    width = GROUPS * CHUNK

    o_kv, o_kr, o_u = LORA, 2 * LORA, 2 * LORA + ROPE
    w1 = jnp.concatenate([w_in[:, :, :o_kr], w_in[:, :, o_u:]], axis=2).astype(BF16)
    w_kr = w_in[:, :, o_kr:o_u]
    wkr2 = jnp.concatenate([w_kr, _rotate_half_cols(w_kr)], axis=2).astype(BF16)
    colgain = jnp.concatenate([q_norm, kv_norm, jnp.ones((depth, width), F32),
                               chunk_v_norm.reshape(depth, width)], axis=1).reshape(depth, 1, -1)
    q_rope_w = w_uq[..., NOPE:]
    wuq2 = jnp.concatenate([w_uq[..., :NOPE], q_rope_w, _rotate_half_cols(q_rope_w)], axis=-1)
    wuq2 = wuq2.transpose(0, 2, 1, 3).astype(BF16)
    wukt = w_uk.transpose(0, 2, 3, 1).astype(BF16)
    wuv_h = w_uv.transpose(0, 2, 1, 3).astype(BF16)
    wuv_all = w_uv.reshape(depth, LORA, HEADS * VDIM).astype(BF16)
    wout = w_out.astype(BF16)
    wmq = w_mem_q.reshape(depth, d, mem_width).astype(BF16)
    wmkv = jnp.concatenate([w_mem_k.reshape(depth, d, mem_width),
                            w_mem_v.reshape(depth, d, mem_width)], axis=2).astype(BF16)
    wmo = w_mem_o.reshape(depth, mem_width, d).astype(BF16)
    wgd, wud, wdd = w_gate_dense.astype(BF16), w_up_dense.astype(BF16), w_down_dense.astype(BF16)
    wgm, wum, wdm = w_gate_moe.astype(BF16), w_up_moe.astype(BF16), w_down_moe.astype(BF16)
    wr = jnp.pad(w_router, ((0, 0), (0, 0), (0, LANES - N_EXPERTS)))
    wr_hi = wr.astype(BF16)
    wr_lo = (wr - wr_hi.astype(F32)).astype(BF16)
    bst = b_spatial.transpose(0, 2, 1)
    mix_a = jnp.repeat(w_spatial[:, :, 0, 0], CHUNK, axis=1).reshape(depth, 1, width)
    mix_c = jnp.repeat(b_spatial[:, :, 0], CHUNK, axis=1).reshape(depth, 1, width)
    cache_k = cache_mem_k.reshape(depth, n_sample, mem_tokens * MEM_HEADS, MEM_DIM)
    cache_v = cache_mem_v.reshape(depth, n_sample, mem_tokens * MEM_HEADS, MEM_DIM)
    cache_krt = jnp.swapaxes(cache_k_rope, 2, 3)

    pos = jnp.concatenate([jnp.tile(jnp.arange(seq), batch), jnp.full((n_sample,), past_len)])
    inv_freq = 1.0 / (ROPE_THETA ** (jnp.arange(ROPE // 2, dtype=F32) / (ROPE // 2)))
    ang = pos.astype(F32)[:, None] * inv_freq[None, :]
    cos = jnp.tile(jnp.cos(ang), (1, 2))
    sin = jnp.tile(jnp.sin(ang), (1, 2))

    def gain(v):
        return v.reshape(1, -1)

    x = jnp.concatenate([x_prompt.reshape(n_prompt, d), x_sample.reshape(n_sample, d)], axis=0)
    mem_flat = mem_prompt.reshape(batch * mem_tokens, d)
    lat_p, kr_p, mk_p, mv_p, lat_s, kr_s, v_s = [], [], [], [], [], [], []
    for l in range(depth):
        z, kr, kcat = _in_proj(l, x, gain(norm_mix[l]), w1, wkr2, colgain, cos, sin)
        q = _q_proj(l, z, wuq2, wukt, cos, sin)
        o_att = _attn_prompt(l, q, kcat, wuv_h, batch, seq)
        qs = q[:, n_prompt:, :].transpose(1, 0, 2)
        ks = kcat[n_prompt:].reshape(n_sample, 1, KCAT)
        o_att = _attn_sample(l, page_table, qs, ks, wuv_all, o_att, cache_kv_latent, cache_krt)
        o_chk = _chunk_prompt(l, z, w_spatial, bst, n_prompt)
        o_chk = _chunk_sample(z, mix_a[l], mix_c[l], o_chk, n_sample)
        x = _out_proj(l, o_att, o_chk, gain(out_norm_mla[l]), gain(out_norm_chunk[l]), wout, x)
        lat_p.append(z[:n_prompt, o_kv:o_kr].reshape(batch, seq, LORA))
        kr_p.append(kr[:n_prompt].reshape(batch, seq, ROPE))
        lat_s.append(z[n_prompt:, o_kv:o_kr].reshape(n_sample, 1, LORA))
        kr_s.append(kr[n_prompt:].reshape(n_sample, 1, ROPE))
        v_s.append(z[n_prompt:, 2 * LORA + width:].reshape(n_sample, 1, width))

        mkv = _norm_mm(l, mem_flat, gain(norm_mem_kv[l]), wmkv, 512, 512)
        mk_p.append(mkv[:, :mem_width].reshape(batch, mem_tokens, MEM_HEADS, MEM_DIM))
        mv_p.append(mkv[:, mem_width:].reshape(batch, mem_tokens, MEM_HEADS, MEM_DIM))
        x_new = _mem_prompt(l, x, gain(norm_mem_q[l]), wmq, mkv, wmo, n_prompt, seq, mem_tokens)
        x = _mem_sample(l, x, x_new, gain(norm_mem_q[l]), wmq, wmo, cache_k, cache_v, n_sample)

        i = l // 2
        if l % 2 == 0:
            act = _ffn_up(i, x, gain(norm_ffn[l]), wgd, wud)
            x = _ffn_down(i, act, wdd, x)
        else:
            x = _moe(i, x, gain(norm_ffn[l]), wr_hi[i], wr_lo[i], wgm, wum, wdm)

    y = _final_norm(x, gain(final_norm))
    return (y[:n_prompt].reshape(batch, seq, d), y[n_prompt:].reshape(n_sample, 1, d),
            jnp.stack(lat_p), jnp.stack(kr_p), jnp.stack(mk_p), jnp.stack(mv_p),
            jnp.stack(lat_s), jnp.stack(kr_s), jnp.stack(v_s))
```

```python
import functools

import jax
import jax.numpy as jnp
from jax import lax
from jax.experimental import pallas as pl
from jax.experimental.pallas import tpu as pltpu

F32 = jnp.float32
BF16 = jnp.bfloat16
U32 = jnp.uint32
I32 = jnp.int32

EPS = 1e-6
ROPE_THETA = 10000.0
ROPE = 64
NOPE = 128
LORA = 512
HEADS = 8
VDIM = 128
KCAT = 640
GROUPS = 8
CHUNK = 128
MEM_HEADS = 4
MEM_DIM = 128
N_EXPERTS = 8
MLA_SCALE = (NOPE + ROPE) ** -0.5
MEM_SCALE = MEM_DIM ** -0.5

LANES = 128
VMEM_CAP_BYTES = 60000 * 1024

TM = 1040
TQ = 256
SUB_PAGES = 16
MEM_SAMPLES_PER_STEP = 8
MOE_TM = 512
ROW_TILE = 208
DMA_UNROLL = 8


def _nbytes(shape, dtype):
    n = 1
    for s in shape:
        n *= s
    return n * jnp.dtype(dtype).itemsize


def _params(sem, blocks, scratch=0, temps=0):
    need = 2 * sum(_nbytes(s, d) for s, d in blocks) + scratch + temps
    return pltpu.CompilerParams(dimension_semantics=sem, vmem_limit_bytes=min(need, VMEM_CAP_BYTES))


def _rms(x, g):
    return x * lax.rsqrt(jnp.mean(x * x, axis=-1, keepdims=True) + EPS) * g


def _gelu(x):
    return x * (0.5 * (1.0 + jnp.tanh(0.7978845608028654 * (x + 0.044715 * (x * x * x)))))


def _sigmoid(x):
    return 1.0 / (1.0 + jnp.exp(-x))


def _dot(a, b):
    return jnp.dot(a, b, preferred_element_type=F32)


def _dot_nt(a, b):
    return lax.dot_general(a, b, (((1,), (1,)), ((), ())), preferred_element_type=F32)


def _in_proj_kernel(x_ref, g_ref, w_ref, wkr_ref, cg_ref, cos_ref, sin_ref,
                    z_ref, kr_ref, kcat_ref, hn_ref):
    j = pl.program_id(1)

    @pl.when(j == 0)
    def _():
        hn = _rms(x_ref[...], g_ref[...]).astype(BF16)
        hn_ref[...] = hn
        kr2 = _dot(hn, wkr_ref[...])
        kr = kr2[:, :ROPE] * cos_ref[...] + kr2[:, ROPE:] * sin_ref[...]
        kr_ref[...] = kr
        kcat_ref[:, LORA:KCAT] = jnp.concatenate([kr, jnp.zeros_like(kr)], axis=-1).astype(BF16)

    acc = _dot(hn_ref[...], w_ref[...])
    cg = cg_ref[...]

    @pl.when(j <= 1)
    def _():
        y = _rms(acc, cg)
        z_ref[...] = y

        @pl.when(j == 1)
        def _():
            kcat_ref[:, 0:LORA] = y.astype(BF16)

    @pl.when((j == 2) | (j == 3))
    def _():
        z_ref[...] = _gelu(acc)

    @pl.when(j >= 4)
    def _():
        gl = _gelu(acc)
        for k in range(LORA // CHUNK):
            sl = slice(k * CHUNK, (k + 1) * CHUNK)
            z_ref[:, sl] = _rms(gl[:, sl], cg[:, sl])


def _in_proj(layer, x, g, w1, wkr, colgain, cos, sin):
    t, d = x.shape
    n = w1.shape[2]
    tn = LORA
    blocks = [((TM, d), F32), ((d, tn), BF16), ((d, 2 * ROPE), BF16), ((TM, tn), F32),
              ((TM, LANES), F32), ((TM, KCAT), BF16), ((TM, 2 * LANES), F32)]
    return pl.pallas_call(
        _in_proj_kernel,
        name="in_proj",
        grid=(t // TM, n // tn),
        in_specs=[
            pl.BlockSpec((TM, d), lambda i, j: (i, 0)),
            pl.BlockSpec((1, d), lambda i, j: (0, 0)),
            pl.BlockSpec((None, d, tn), lambda i, j: (layer, 0, j)),
            pl.BlockSpec((None, d, 2 * ROPE), lambda i, j: (layer, 0, 0)),
            pl.BlockSpec((None, 1, tn), lambda i, j: (layer, 0, j)),
            pl.BlockSpec((TM, ROPE), lambda i, j: (i, 0)),
            pl.BlockSpec((TM, ROPE), lambda i, j: (i, 0)),
        ],
        out_specs=[
            pl.BlockSpec((TM, tn), lambda i, j: (i, j)),
            pl.BlockSpec((TM, ROPE), lambda i, j: (i, 0)),
            pl.BlockSpec((TM, KCAT), lambda i, j: (i, 0)),
        ],
        out_shape=[
            jax.ShapeDtypeStruct((t, n), F32),
            jax.ShapeDtypeStruct((t, ROPE), F32),
            jax.ShapeDtypeStruct((t, KCAT), BF16),
        ],
        scratch_shapes=[pltpu.VMEM((TM, d), BF16)],
        compiler_params=_params(("parallel", "arbitrary"), blocks,
                                scratch=_nbytes((TM, d), BF16), temps=3 * _nbytes((TM, d), F32)),
    )(x, g, w1, wkr, colgain, cos, sin)


def _q_proj_kernel(cq_ref, wuq_ref, wuk_ref, cos_ref, sin_ref, q_ref):
    cq = cq_ref[...].astype(BF16)
    q = _dot(cq, wuq_ref[...])
    qa = _dot(q[:, :NOPE].astype(BF16), wuk_ref[...])
    qr = q[:, NOPE:NOPE + ROPE] * cos_ref[...] + q[:, NOPE + ROPE:] * sin_ref[...]
    q_ref[:, 0:LORA] = (qa * MLA_SCALE).astype(BF16)
    q_ref[:, LORA:KCAT] = jnp.concatenate([qr * MLA_SCALE, jnp.zeros_like(qr)], axis=-1).astype(BF16)


def _q_proj(layer, z, wuq2, wukt, cos, sin):
    t = z.shape[0]
    blocks = [((TM, LORA), F32), ((LORA, 2 * NOPE), BF16), ((NOPE, LORA), BF16),
              ((TM, LANES), F32), ((TM, LANES), F32), ((TM, KCAT), BF16)]
    return pl.pallas_call(
        _q_proj_kernel,
        name="q_proj",
        grid=(t // TM, HEADS),
        in_specs=[
            pl.BlockSpec((TM, LORA), lambda i, h: (i, 0)),
            pl.BlockSpec((None, None, LORA, 2 * NOPE), lambda i, h: (layer, h, 0, 0)),
            pl.BlockSpec((None, None, NOPE, LORA), lambda i, h: (layer, h, 0, 0)),
            pl.BlockSpec((TM, ROPE), lambda i, h: (i, 0)),
            pl.BlockSpec((TM, ROPE), lambda i, h: (i, 0)),
        ],
        out_specs=pl.BlockSpec((None, TM, KCAT), lambda i, h: (h, i, 0)),
        out_shape=jax.ShapeDtypeStruct((HEADS, t, KCAT), BF16),
        compiler_params=_params(("parallel", "arbitrary"), blocks, temps=4 * _nbytes((TM, LORA), F32)),
    )(z, wuq2, wukt, cos, sin)


def _attn_prompt_kernel(q_ref, k_ref, wuv_ref, o_ref, m_ref, l_ref, acc_ref, *, nq):
    step = pl.program_id(0)
    n_steps = pl.num_programs(0) - 1
    qi = step % nq
    rows = HEADS * TQ

    @pl.when(step < n_steps)
    def _():
        q = q_ref[...].reshape(rows, KCAT)
        m_ref[...] = jnp.full((rows, 1), -jnp.inf, F32)
        l_ref[...] = jnp.zeros((rows, 1), F32)
        acc_ref[...] = jnp.zeros((rows, LORA), F32)

        def chunk(c):
            return k_ref[pl.ds(pl.multiple_of(c * TQ, TQ), TQ), :]

        def update(s, kc):
            m_prev = m_ref[...]
            m_new = jnp.maximum(m_prev, jnp.max(s, axis=-1, keepdims=True))
            alpha = jnp.exp(m_prev - m_new)
            p = jnp.exp(s - m_new)
            l_ref[...] = alpha * l_ref[...] + jnp.sum(p, axis=-1, keepdims=True)
            acc_ref[...] = alpha * acc_ref[...] + _dot(p.astype(BF16), kc[:, :LORA])
            m_ref[...] = m_new

        def body(c, s):
            s_next = _dot_nt(q, chunk(c + 1))
            update(s, chunk(c))
            return s_next

        s = lax.fori_loop(0, qi, body, _dot_nt(q, chunk(0)))
        q_pos = lax.broadcasted_iota(I32, (rows, TQ), 0) & (TQ - 1)
        k_pos = lax.broadcasted_iota(I32, (rows, TQ), 1)
        update(jnp.where(k_pos <= q_pos, s, -jnp.inf), chunk(qi))
        o = acc_ref[...] * (1.0 / l_ref[...])
        for h in range(HEADS):
            oh = o[h * TQ:(h + 1) * TQ, :].astype(BF16)
            o_ref[:, h * VDIM:(h + 1) * VDIM] = _dot(oh, wuv_ref[h])

    @pl.when(step == n_steps)
    def _():
        o_ref[...] = jnp.zeros(o_ref.shape, F32)


def _attn_prompt(layer, q, kcat, wuv_h, batch, seq):
    t = kcat.shape[0]
    nq = seq // TQ
    n_steps = batch * nq
    rows = HEADS * TQ
    blocks = [((HEADS, TQ, KCAT), BF16), ((seq, KCAT), BF16), ((HEADS, LORA, VDIM), BF16),
              ((TQ, HEADS * VDIM), F32)]
    scratch = 2 * _nbytes((rows, LANES), F32) + _nbytes((rows, LORA), F32)
    return pl.pallas_call(
        functools.partial(_attn_prompt_kernel, nq=nq),
        name="attn_prompt",
        grid=(n_steps + 1,),
        in_specs=[
            pl.BlockSpec((HEADS, TQ, KCAT), lambda i: (0, jnp.minimum(i, n_steps - 1), 0)),
            pl.BlockSpec((seq, KCAT), lambda i: (jnp.minimum(i, n_steps - 1) // nq, 0)),
            pl.BlockSpec((None, HEADS, LORA, VDIM), lambda i: (layer, 0, 0, 0)),
        ],
        out_specs=pl.BlockSpec((TQ, HEADS * VDIM), lambda i: (i, 0)),
        out_shape=jax.ShapeDtypeStruct((t, HEADS * VDIM), F32),
        scratch_shapes=[pltpu.VMEM((rows, 1), F32), pltpu.VMEM((rows, 1), F32),
                        pltpu.VMEM((rows, LORA), F32)],
        compiler_params=_params(("arbitrary",), blocks, scratch=scratch,
                                temps=6 * _nbytes((rows, TQ), F32) + 2 * _nbytes((rows, LORA), F32)),
    )(q, kcat, wuv_h)


def _attn_sample_kernel(pt_ref, q_ref, ks_ref, wuv_ref, lat_hbm, krt_hbm, oin_ref, o_ref,
                        latbuf_ref, krtbuf_ref, sem, *, layer, n_pages):
    del oin_ref
    b = pl.program_id(0)
    slot = b % 2
    page = lat_hbm.shape[2]
    keys = SUB_PAGES * page

    def page_copies(sample, dst_slot):
        copies = []
        for g in range(n_pages):
            pg = pt_ref[sample * n_pages + g]
            span = pl.ds(g * page, page)
            copies.append(pltpu.make_async_copy(lat_hbm.at[layer, pg], latbuf_ref.at[dst_slot, span, :],
                                                sem.at[0, dst_slot]))
            copies.append(pltpu.make_async_copy(krt_hbm.at[layer, pg], krtbuf_ref.at[dst_slot, :, span],
                                                sem.at[1, dst_slot]))
        return copies

    @pl.when(b == 0)
    def _():
        for c in page_copies(b, slot):
            c.start()

    @pl.when(b + 1 < pl.num_programs(0))
    def _():
        for c in page_copies(b + 1, 1 - slot):
            c.start()

    for c in page_copies(b, slot):
        c.wait()

    q = q_ref[0]
    qa = jnp.concatenate([q[:, :LORA], jnp.zeros((LANES - HEADS, LORA), BF16)], axis=0)
    qr = q[:, LORA:LORA + ROPE]
    n_sub = n_pages // SUB_PAGES
    lats = [latbuf_ref[slot, j * keys:(j + 1) * keys, :].astype(BF16) for j in range(n_sub)]
    s_ts = [_dot_nt(lats[j], qa) for j in range(n_sub)]
    parts = []
    for j in range(n_sub):
        krt = krtbuf_ref[slot, :, j * keys:(j + 1) * keys].astype(BF16)
        s = s_ts[j].T[:HEADS, :] + _dot(qr, krt)
        m_j = jnp.max(s, axis=-1, keepdims=True)
        p = jnp.exp(s - m_j)
        parts.append((m_j, jnp.sum(p, axis=-1, keepdims=True), _dot(p.astype(BF16), lats[j])))

    q = q.astype(F32)

    k_self = ks_ref[0].astype(F32)
    s_self = jnp.sum(q * k_self, axis=-1, keepdims=True)
    m = s_self
    for m_j, _, _ in parts:
        m = jnp.maximum(m, m_j)
    w_self = jnp.exp(s_self - m)
    l = w_self
    acc = w_self * k_self[:, :LORA]
    for m_j, l_j, acc_j in parts:
        a = jnp.exp(m_j - m)
        l = l + a * l_j
        acc = acc + a * acc_j
    o_lat = acc * (1.0 / l)
    r = _dot(o_lat.astype(BF16), wuv_ref[...])
    head = lax.broadcasted_iota(I32, r.shape, 0)
    col_head = lax.broadcasted_iota(I32, r.shape, 1) // VDIM
    o_ref[pl.ds(b, 1), :] = jnp.sum(jnp.where(head == col_head, r, 0.0), axis=0, keepdims=True)


def _attn_sample(layer, page_table, qs, ks, wuv_all, o_att, cache_lat, cache_krt):
    nb, n_pages = page_table.shape
    page = cache_lat.shape[2]
    past = n_pages * page
    t = o_att.shape[0]
    row_block = (t - nb) // nb
    pt_flat = page_table.reshape(-1)
    keys = SUB_PAGES * page
    blocks = [((HEADS, KCAT), BF16), ((1, KCAT), BF16), ((LORA, HEADS * VDIM), BF16),
              ((nb, HEADS * VDIM), F32)]
    scratch = _nbytes((2, past, LORA), F32) + _nbytes((2, ROPE, past), F32)
    grid_spec = pltpu.PrefetchScalarGridSpec(
        num_scalar_prefetch=1,
        grid=(nb,),
        in_specs=[
            pl.BlockSpec((1, HEADS, KCAT), lambda b, pt: (b, 0, 0)),
            pl.BlockSpec((1, 1, KCAT), lambda b, pt: (b, 0, 0)),
            pl.BlockSpec((None, LORA, HEADS * VDIM), lambda b, pt: (layer, 0, 0)),
            pl.BlockSpec(memory_space=pl.ANY),
            pl.BlockSpec(memory_space=pl.ANY),
            pl.BlockSpec(memory_space=pl.ANY),
        ],
        out_specs=pl.BlockSpec((nb, HEADS * VDIM), lambda b, pt: (row_block, 0)),
        scratch_shapes=[pltpu.VMEM((2, past, LORA), F32), pltpu.VMEM((2, ROPE, past), F32),
                        pltpu.SemaphoreType.DMA((2, 2))],
    )
    return pl.pallas_call(
        functools.partial(_attn_sample_kernel, layer=layer, n_pages=n_pages),
        name="attn_sample",
        grid_spec=grid_spec,
        out_shape=jax.ShapeDtypeStruct(o_att.shape, F32),
        input_output_aliases={6: 0},
        compiler_params=_params(("arbitrary",), blocks, scratch=scratch,
                                temps=(n_pages // SUB_PAGES + 2) * (_nbytes((keys, LANES), F32)
                                                                    + _nbytes((keys, LORA), BF16))),
    )(pt_flat, qs, ks, wuv_all, cache_lat, cache_krt, o_att)


def _chunk_prompt_kernel(u_ref, v_ref, ws_ref, bst_ref, o_ref):
    step = pl.program_id(0)
    n_steps = pl.num_programs(0) - 1

    @pl.when(step < n_steps)
    def _():
        row = lax.broadcasted_iota(I32, (CHUNK, CHUNK), 0)
        col = lax.broadcasted_iota(I32, (CHUNK, CHUNK), 1)
        n_chunks = u_ref.shape[0] // CHUNK
        for g in range(GROUPS):
            w = jnp.where(row >= col, ws_ref[g], 0.0).astype(BF16)
            bias = bst_ref[:, g:g + 1]
            cs = slice(g * CHUNK, (g + 1) * CHUNK)
            for n in range(n_chunks):
                rs = slice(n * CHUNK, (n + 1) * CHUNK)
                mixed = _dot(w, v_ref[rs, cs].astype(BF16)) + bias
                o_ref[rs, cs] = u_ref[rs, cs] * mixed

    @pl.when(step == n_steps)
    def _():
        o_ref[...] = jnp.zeros(o_ref.shape, F32)


def _chunk_prompt(layer, z, ws, bst, n_prompt):
    t = z.shape[0]
    width = GROUPS * CHUNK
    rows = 4 * CHUNK
    n_steps = n_prompt // rows
    blocks = [((rows, width), F32)] * 3 + [((GROUPS, CHUNK, CHUNK), F32), ((CHUNK, LANES), F32)]
    return pl.pallas_call(
        _chunk_prompt_kernel,
        name="chunk_prompt",
        grid=(n_steps + 1,),
        in_specs=[
            pl.BlockSpec((rows, width), lambda i: (jnp.minimum(i, n_steps - 1), 1)),
            pl.BlockSpec((rows, width), lambda i: (jnp.minimum(i, n_steps - 1), 2)),
            pl.BlockSpec((None, GROUPS, CHUNK, CHUNK), lambda i: (layer, 0, 0, 0)),
            pl.BlockSpec((None, CHUNK, GROUPS), lambda i: (layer, 0, 0)),
        ],
        out_specs=pl.BlockSpec((rows, width), lambda i: (i, 0)),
        out_shape=jax.ShapeDtypeStruct((t, width), F32),
        compiler_params=_params(("arbitrary",), blocks, temps=_nbytes((rows, width), F32)),
    )(z, z, ws, bst)


def _chunk_sample_kernel(u_ref, v_ref, a_ref, c_ref, oin_ref, o_ref):
    del oin_ref
    o_ref[...] = u_ref[...] * (a_ref[...] * v_ref[...] + c_ref[...])


def _chunk_sample(z, a, c, o_chk, n_sample):
    t = z.shape[0]
    width = GROUPS * CHUNK
    rb = (t - n_sample) // n_sample
    blocks = [((n_sample, width), F32)] * 3 + [((8, width), F32)] * 2
    return pl.pallas_call(
        _chunk_sample_kernel,
        name="chunk_sample",
        grid=(1,),
        in_specs=[
            pl.BlockSpec((n_sample, width), lambda i: (rb, 1)),
            pl.BlockSpec((n_sample, width), lambda i: (rb, 2)),
            pl.BlockSpec((1, width), lambda i: (0, 0)),
            pl.BlockSpec((1, width), lambda i: (0, 0)),
            pl.BlockSpec(memory_space=pl.ANY),
        ],
        out_specs=pl.BlockSpec((n_sample, width), lambda i: (rb, 0)),
        out_shape=jax.ShapeDtypeStruct(o_chk.shape, F32),
        input_output_aliases={4: 0},
        compiler_params=_params(("arbitrary",), blocks),
    )(z, z, a, c, o_chk)


def _out_proj_kernel(oa_ref, oc_ref, ga_ref, gc_ref, w_ref, x_ref, o_ref, cat_ref):
    half = oa_ref.shape[1]

    @pl.when(pl.program_id(1) == 0)
    def _():
        cat_ref[:, 0:half] = _rms(oa_ref[...], ga_ref[...]).astype(BF16)
        cat_ref[:, half:] = _rms(oc_ref[...], gc_ref[...]).astype(BF16)

    o_ref[...] = x_ref[...] + _dot(cat_ref[...], w_ref[...].astype(BF16))


def _out_proj(layer, o_att, o_chk, ga, gc, w, x):
    t, d = x.shape
    half = o_att.shape[1]
    tn = 512
    blocks = [((TM, half), F32)] * 2 + [((d, tn), F32), ((TM, tn), F32), ((TM, tn), F32)]
    return pl.pallas_call(
        _out_proj_kernel,
        name="out_proj",
        grid=(t // TM, d // tn),
        in_specs=[
            pl.BlockSpec((TM, half), lambda i, j: (i, 0)),
            pl.BlockSpec((TM, half), lambda i, j: (i, 0)),
            pl.BlockSpec((1, half), lambda i, j: (0, 0)),
            pl.BlockSpec((1, half), lambda i, j: (0, 0)),
            pl.BlockSpec((None, d, tn), lambda i, j: (layer, 0, j)),
            pl.BlockSpec((TM, tn), lambda i, j: (i, j)),
        ],
        out_specs=pl.BlockSpec((TM, tn), lambda i, j: (i, j)),
        out_shape=jax.ShapeDtypeStruct((t, d), F32),
        scratch_shapes=[pltpu.VMEM((TM, d), BF16)],
        compiler_params=_params(("parallel", "arbitrary"), blocks, scratch=_nbytes((TM, d), BF16),
                                temps=3 * _nbytes((TM, half), F32)),
    )(o_att, o_chk, ga, gc, w, x)


def _norm_mm_kernel(x_ref, g_ref, w_ref, o_ref, hn_ref):
    @pl.when(pl.program_id(1) == 0)
    def _():
        hn_ref[...] = _rms(x_ref[...], g_ref[...]).astype(BF16)

    o_ref[...] = _dot(hn_ref[...], w_ref[...])


def _norm_mm(layer, x, g, w, tm, tn):
    m, d = x.shape
    n = w.shape[2]
    blocks = [((tm, d), F32), ((d, tn), BF16), ((tm, tn), F32)]
    return pl.pallas_call(
        _norm_mm_kernel,
        name="mem_kv",
        grid=(m // tm, n // tn),
        in_specs=[
            pl.BlockSpec((tm, d), lambda i, j: (i, 0)),
            pl.BlockSpec((1, d), lambda i, j: (0, 0)),
            pl.BlockSpec((None, d, tn), lambda i, j: (layer, 0, j)),
        ],
        out_specs=pl.BlockSpec((tm, tn), lambda i, j: (i, j)),
        out_shape=jax.ShapeDtypeStruct((m, n), F32),
        scratch_shapes=[pltpu.VMEM((tm, d), BF16)],
        compiler_params=_params(("parallel", "arbitrary"), blocks, scratch=_nbytes((tm, d), BF16),
                                temps=3 * _nbytes((tm, d), F32)),
    )(x, g, w)


def _mem_prompt_kernel(x_ref, g_ref, wq_ref, kv_ref, wo_ref, o_ref):
    step = pl.program_id(0)
    n_steps = pl.num_programs(0) - 1

    @pl.when(step < n_steps)
    def _():
        x = x_ref[...]
        q = _dot(_rms(x, g_ref[...]).astype(BF16), wq_ref[...])
        width = MEM_HEADS * MEM_DIM
        outs = []
        for h in range(MEM_HEADS):
            hs = slice(h * MEM_DIM, (h + 1) * MEM_DIM)
            vs = slice(width + h * MEM_DIM, width + (h + 1) * MEM_DIM)
            s = _dot_nt(q[:, hs].astype(BF16), kv_ref[:, hs].astype(BF16)) * MEM_SCALE
            p = jnp.exp(s - jnp.max(s, axis=-1, keepdims=True))
            p = p * (1.0 / jnp.sum(p, axis=-1, keepdims=True))
            outs.append(_dot(p.astype(BF16), kv_ref[:, vs].astype(BF16)))
        o = jnp.concatenate(outs, axis=-1).astype(BF16)
        o_ref[...] = x + _dot(o, wo_ref[...])

    @pl.when(step == n_steps)
    def _():
        o_ref[...] = jnp.zeros(o_ref.shape, F32)


def _mem_prompt(layer, x, g, wq, mkv, wo, n_prompt, seq, mem_tokens):
    t, d = x.shape
    rows = 512
    width = MEM_HEADS * MEM_DIM
    per_batch = seq // rows
    n_steps = n_prompt // rows
    blocks = [((rows, d), F32), ((d, width), BF16), ((mem_tokens, 2 * width), F32),
              ((width, d), BF16), ((rows, d), F32)]
    return pl.pallas_call(
        _mem_prompt_kernel,
        name="mem_prompt",
        grid=(n_steps + 1,),
        in_specs=[
            pl.BlockSpec((rows, d), lambda i: (jnp.minimum(i, n_steps - 1), 0)),
            pl.BlockSpec((1, d), lambda i: (0, 0)),
            pl.BlockSpec((None, d, width), lambda i: (layer, 0, 0)),
            pl.BlockSpec((mem_tokens, 2 * width), lambda i: (jnp.minimum(i, n_steps - 1) // per_batch, 0)),
            pl.BlockSpec((None, width, d), lambda i: (layer, 0, 0)),
        ],
        out_specs=pl.BlockSpec((rows, d), lambda i: (i, 0)),
        out_shape=jax.ShapeDtypeStruct((t, d), F32),
        compiler_params=_params(("arbitrary",), blocks, temps=4 * _nbytes((rows, d), F32)),
    )(x, g, wq, mkv, wo)


def _mem_sample_kernel(x_ref, g_ref, wq_ref, k_ref, v_ref, wo_ref, xin_ref, o_ref, q_scr, o_scr):
    del xin_ref
    step = pl.program_id(0)
    width = MEM_HEADS * MEM_DIM

    @pl.when(step == 0)
    def _():
        q_scr[...] = _dot(_rms(x_ref[...], g_ref[...]).astype(BF16), wq_ref[...])

    kv_rows = k_ref.shape[1]
    row_head = lax.broadcasted_iota(I32, (8, kv_rows), 1) & (MEM_HEADS - 1)
    sub = lax.broadcasted_iota(I32, (8, kv_rows), 0)
    own = (row_head == sub) | (sub >= MEM_HEADS)
    pad = jnp.zeros((8 - MEM_HEADS, MEM_DIM), F32)
    for n in range(MEM_SAMPLES_PER_STEP):
        r = step * MEM_SAMPLES_PER_STEP + n
        q_row = q_scr[pl.ds(r, 1), :]
        q_heads = jnp.concatenate([q_row[:, h * MEM_DIM:(h + 1) * MEM_DIM] for h in range(MEM_HEADS)]
                                  + [pad], axis=0)
        s = _dot_nt(q_heads.astype(BF16), k_ref[n].astype(BF16)) * MEM_SCALE
        s = jnp.where(own, s, -jnp.inf)
        p = jnp.exp(s - jnp.max(s, axis=-1, keepdims=True))
        p = p * (1.0 / jnp.sum(p, axis=-1, keepdims=True))
        ob = _dot(p.astype(BF16), v_ref[n].astype(BF16))
        o_scr[pl.ds(r, 1), :] = jnp.concatenate([ob[h:h + 1, :] for h in range(MEM_HEADS)], axis=1)

    @pl.when(step == pl.num_programs(0) - 1)
    def _():
        o_ref[...] = x_ref[...] + _dot(o_scr[...].astype(BF16), wo_ref[...])


def _mem_sample(layer, x_mid, x_new, g, wq, wo, cache_k, cache_v, n_sample):
    t, d = x_mid.shape
    width = MEM_HEADS * MEM_DIM
    kv_rows = cache_k.shape[2]
    per = MEM_SAMPLES_PER_STEP
    rb = (t - n_sample) // n_sample
    blocks = [((n_sample, d), F32), ((d, width), BF16), ((per, kv_rows, MEM_DIM), F32),
              ((per, kv_rows, MEM_DIM), F32), ((width, d), BF16), ((n_sample, d), F32)]
    return pl.pallas_call(
        _mem_sample_kernel,
        name="mem_sample",
        grid=(n_sample // per,),
        in_specs=[
            pl.BlockSpec((n_sample, d), lambda i: (rb, 0)),
            pl.BlockSpec((1, d), lambda i: (0, 0)),
            pl.BlockSpec((None, d, width), lambda i: (layer, 0, 0)),
            pl.BlockSpec((None, per, kv_rows, MEM_DIM), lambda i: (layer, i, 0, 0)),
            pl.BlockSpec((None, per, kv_rows, MEM_DIM), lambda i: (layer, i, 0, 0)),
            pl.BlockSpec((None, width, d), lambda i: (layer, 0, 0)),
            pl.BlockSpec(memory_space=pl.ANY),
        ],
        out_specs=pl.BlockSpec((n_sample, d), lambda i: (rb, 0)),
        out_shape=jax.ShapeDtypeStruct((t, d), F32),
        scratch_shapes=[pltpu.VMEM((n_sample, width), F32), pltpu.VMEM((n_sample, width), F32)],
        input_output_aliases={6: 0},
        compiler_params=_params(("arbitrary",), blocks, scratch=2 * _nbytes((n_sample, width), F32),
                                temps=4 * _nbytes((n_sample, d), F32)),
    )(x_mid, g, wq, cache_k, cache_v, wo, x_new)


def _ffn_up_kernel(x_ref, g_ref, wg_ref, wu_ref, o_ref, hn_ref):
    @pl.when(pl.program_id(1) == 0)
    def _():
        hn_ref[...] = _rms(x_ref[...], g_ref[...]).astype(BF16)

    hn = hn_ref[...]
    a = _dot(hn, wg_ref[...].astype(BF16))
    b = _dot(hn, wu_ref[...].astype(BF16))
    o_ref[...] = (a * _sigmoid(a) * b).astype(BF16)


def _ffn_up(layer, x, g, wg, wu):
    t, d = x.shape
    f = wg.shape[2]
    tn = 512
    blocks = [((TM, d), F32), ((d, tn), F32), ((d, tn), F32), ((TM, tn), BF16)]
    return pl.pallas_call(
        _ffn_up_kernel,
        name="ffn_up",
        grid=(t // TM, f // tn),
        in_specs=[
            pl.BlockSpec((TM, d), lambda i, j: (i, 0)),
            pl.BlockSpec((1, d), lambda i, j: (0, 0)),
            pl.BlockSpec((None, d, tn), lambda i, j: (layer, 0, j)),
            pl.BlockSpec((None, d, tn), lambda i, j: (layer, 0, j)),
        ],
        out_specs=pl.BlockSpec((TM, tn), lambda i, j: (i, j)),
        out_shape=jax.ShapeDtypeStruct((t, f), BF16),
        scratch_shapes=[pltpu.VMEM((TM, d), BF16)],
        compiler_params=_params(("parallel", "arbitrary"), blocks, scratch=_nbytes((TM, d), BF16),
                                temps=3 * _nbytes((TM, d), F32)),
    )(x, g, wg, wu)


def _mm_res_kernel(a_ref, w_ref, x_ref, o_ref):
    o_ref[...] = x_ref[...] + _dot(a_ref[...], w_ref[...].astype(BF16))


def _ffn_down(layer, act, wd, x):
    t, f = act.shape
    d = wd.shape[2]
    tm, tn = 640, 512
    blocks = [((tm, f), BF16), ((f, tn), F32), ((f, tn), BF16), ((tm, tn), F32), ((tm, tn), F32)]
    return pl.pallas_call(
        _mm_res_kernel,
        name="ffn_down",
        grid=(t // tm, d // tn),
        in_specs=[
            pl.BlockSpec((tm, f), lambda i, j: (i, 0)),
            pl.BlockSpec((None, f, tn), lambda i, j: (layer, 0, j)),
            pl.BlockSpec((tm, tn), lambda i, j: (i, j)),
        ],
        out_specs=pl.BlockSpec((tm, tn), lambda i, j: (i, j)),
        out_shape=jax.ShapeDtypeStruct((t, d), F32),
        compiler_params=_params(("parallel", "arbitrary"), blocks, temps=2 * _nbytes((tm, tn), F32)),
    )(act, wd, x)


def _router_kernel(x_ref, g_ref, whi_ref, wlo_ref, idx_ref, wgt_ref):
    h = _rms(x_ref[...], g_ref[...])
    h_hi = h.astype(BF16)
    h_lo = (h - h_hi.astype(F32)).astype(BF16)
    logits = _dot(h_hi, whi_ref[...]) + (_dot(h_lo, whi_ref[...]) + _dot(h_hi, wlo_ref[...]))
    lane = lax.broadcasted_iota(I32, logits.shape, 1)
    logits = jnp.where(lane < N_EXPERTS, logits, -jnp.inf)
    v1 = jnp.max(logits, axis=-1, keepdims=True)
    i1 = jnp.min(jnp.where(logits == v1, lane, LANES), axis=-1, keepdims=True)
    rest = jnp.where(lane == i1, -jnp.inf, logits)
    v2 = jnp.max(rest, axis=-1, keepdims=True)
    i2 = jnp.min(jnp.where(rest == v2, lane, LANES), axis=-1, keepdims=True)
    e2 = jnp.exp(v2 - v1)
    w1 = 1.0 / (1.0 + e2)
    w2 = e2 * w1
    idx_ref[...] = jnp.where(lane == 0, i1, jnp.where(lane == 1, i2, 0))
    wgt_ref[...] = jnp.where(lane == 0, w1, jnp.where(lane == 1, w2, 0.0))


def _router(x, g, whi, wlo):
    t, d = x.shape
    blocks = [((TM, d), F32), ((d, LANES), BF16), ((d, LANES), BF16), ((TM, LANES), I32), ((TM, LANES), F32)]
    return pl.pallas_call(
        _router_kernel,
        name="router",
        grid=(t // TM,),
        in_specs=[
            pl.BlockSpec((TM, d), lambda i: (i, 0)),
            pl.BlockSpec((1, d), lambda i: (0, 0)),
            pl.BlockSpec((d, LANES), lambda i: (0, 0)),
            pl.BlockSpec((d, LANES), lambda i: (0, 0)),
        ],
        out_specs=[pl.BlockSpec((TM, LANES), lambda i: (i, 0)), pl.BlockSpec((TM, LANES), lambda i: (i, 0))],
        out_shape=[jax.ShapeDtypeStruct((t, LANES), I32), jax.ShapeDtypeStruct((t, LANES), F32)],
        compiler_params=_params(("parallel",), blocks, temps=5 * _nbytes((TM, d), F32)),
    )(x, g, whi, wlo)


def _dispatch_kernel(pos_ref, x_ref, g_ref, xs_in_ref, xs_ref, buf_ref, sem):
    del xs_in_ref
    i = pl.program_id(0)
    rows, d = x_ref.shape
    half = d // 2
    slot = i % 2
    h = _rms(x_ref[...], g_ref[...]).astype(BF16).astype(F32)
    lo = pltpu.bitcast(h[:, :half], U32) >> jnp.uint32(16)
    hi = pltpu.bitcast(h[:, half:], U32)
    buf_ref[slot] = lo | hi

    def row_copy(step, sl, r, k):
        dst = pos_ref[2 * (step * rows + r) + k]
        return pltpu.make_async_copy(buf_ref.at[sl, pl.ds(r, 1), :], xs_ref.at[pl.ds(dst, 1), :], sem.at[sl])

    def start_rows(step, sl):
        def body(r, carry):
            row_copy(step, sl, r, 0).start()
            row_copy(step, sl, r, 1).start()
            return carry
        lax.fori_loop(0, rows, body, 0, unroll=DMA_UNROLL)

    def wait_rows(step, sl):
        def body(r, carry):
            row_copy(step, sl, r, 0).wait()
            row_copy(step, sl, r, 1).wait()
            return carry
        lax.fori_loop(0, rows, body, 0, unroll=DMA_UNROLL)

    start_rows(i, slot)

    @pl.when(i > 0)
    def _():
        wait_rows(i - 1, 1 - slot)

    @pl.when(i == pl.num_programs(0) - 1)
    def _():
        wait_rows(i, slot)


def _dispatch(pos, x, g, xs_init):
    t, d = x.shape
    blocks = [((ROW_TILE, d), F32)]
    grid_spec = pltpu.PrefetchScalarGridSpec(
        num_scalar_prefetch=1,
        grid=(t // ROW_TILE,),
        in_specs=[
            pl.BlockSpec((ROW_TILE, d), lambda i, pos: (i, 0)),
            pl.BlockSpec((1, d), lambda i, pos: (0, 0)),
            pl.BlockSpec(memory_space=pl.ANY),
        ],
        out_specs=pl.BlockSpec(memory_space=pl.ANY),
        scratch_shapes=[pltpu.VMEM((2, ROW_TILE, d // 2), U32), pltpu.SemaphoreType.DMA((2,))],
    )
    return pl.pallas_call(
        _dispatch_kernel,
        name="moe_dispatch",
        grid_spec=grid_spec,
        out_shape=jax.ShapeDtypeStruct(xs_init.shape, U32),
        input_output_aliases={3: 0},
        compiler_params=_params(("arbitrary",), blocks, scratch=_nbytes((2, ROW_TILE, d // 2), U32),
                                temps=4 * _nbytes((ROW_TILE, d), F32)),
    )(pos, x, g, xs_init)


def _unpack_pairs(u):
    lo = pltpu.bitcast(u << jnp.uint32(16), F32).astype(BF16)
    hi = pltpu.bitcast(u & jnp.uint32(0xFFFF0000), F32).astype(BF16)
    return lo, hi


def _moe_proj_kernel(te_ref, nt_ref, x_ref, w_ref, *rest, gated):
    del te_ref
    o_ref = rest[-1]
    i = pl.program_id(1)
    half = x_ref.shape[1]

    @pl.when(i < nt_ref[0])
    def _():
        lo, hi = _unpack_pairs(x_ref[...])
        y = _dot(lo, w_ref[0:half, :].astype(BF16)) + _dot(hi, w_ref[half:, :].astype(BF16))
        if gated:
            a = rest[0][...]
            o_ref[...] = (a * _sigmoid(a) * y).astype(o_ref.dtype)
        else:
            o_ref[...] = y

    @pl.when(i >= nt_ref[0])
    def _():
        o_ref[...] = jnp.zeros(o_ref.shape, o_ref.dtype)


def _moe_proj(layer, tile_expert, n_tiles, xs, w, gate_pre=None):
    r, half = xs.shape
    d, f = w.shape[2], w.shape[3]
    tf = f // 2
    gated = gate_pre is not None
    out_dtype = BF16 if gated else F32
    blocks = [((MOE_TM, half), U32), ((d, tf), F32), ((MOE_TM, tf), F32), ((MOE_TM, tf), out_dtype)]
    in_specs = [
        pl.BlockSpec((MOE_TM, half), lambda j, i, te, nt: (i, 0)),
        pl.BlockSpec((None, None, d, tf), lambda j, i, te, nt: (layer, te[i], 0, j)),
    ]
    operands = [tile_expert, n_tiles, xs, w]
    if gated:
        in_specs.append(pl.BlockSpec((MOE_TM, tf), lambda j, i, te, nt: (i, j)))
        operands.append(gate_pre)
    grid_spec = pltpu.PrefetchScalarGridSpec(
        num_scalar_prefetch=2,
        grid=(f // tf, r // MOE_TM),
        in_specs=in_specs,
        out_specs=pl.BlockSpec((MOE_TM, tf), lambda j, i, te, nt: (i, j)),
    )
    return pl.pallas_call(
        functools.partial(_moe_proj_kernel, gated=gated),
        name="moe_up" if gated else "moe_gate",
        grid_spec=grid_spec,
        out_shape=jax.ShapeDtypeStruct((r, f), out_dtype),
        compiler_params=_params(("arbitrary", "arbitrary"), blocks,
                                temps=_nbytes((d, tf), BF16) + 3 * _nbytes((MOE_TM, tf), F32)),
    )(*operands)


def _moe_down_kernel(te_ref, nt_ref, a_ref, w_ref, o_ref):
    del te_ref
    i = pl.program_id(1)

    @pl.when(i < nt_ref[0])
    def _():
        o_ref[...] = _dot(a_ref[...], w_ref[...].astype(BF16))

    @pl.when(i >= nt_ref[0])
    def _():
        o_ref[...] = jnp.zeros(o_ref.shape, F32)


def _moe_down(layer, tile_expert, n_tiles, act, wd):
    r, f = act.shape
    d = wd.shape[3]
    tn = d // 2
    blocks = [((MOE_TM, f), BF16), ((f, tn), F32), ((MOE_TM, tn), F32)]
    grid_spec = pltpu.PrefetchScalarGridSpec(
        num_scalar_prefetch=2,
        grid=(d // tn, r // MOE_TM),
        in_specs=[
            pl.BlockSpec((MOE_TM, f), lambda j, i, te, nt: (i, 0)),
            pl.BlockSpec((None, None, f, tn), lambda j, i, te, nt: (layer, te[i], 0, j)),
        ],
        out_specs=pl.BlockSpec((MOE_TM, tn), lambda j, i, te, nt: (i, j)),
    )
    return pl.pallas_call(
        _moe_down_kernel,
        name="moe_down",
        grid_spec=grid_spec,
        out_shape=jax.ShapeDtypeStruct((r, d), F32),
        compiler_params=_params(("arbitrary", "arbitrary"), blocks,
                                temps=_nbytes((f, tn), BF16) + 2 * _nbytes((MOE_TM, tn), F32)),
    )(tile_expert, n_tiles, act, wd)


def _combine_kernel(pos_ref, x_ref, wgt_ref, y_ref, o_ref, buf_ref, sem):
    i = pl.program_id(0)
    rows = x_ref.shape[0]
    slot = i % 2

    def row_copy(step, sl, r, k):
        src = pos_ref[2 * (step * rows + r) + k]
        return pltpu.make_async_copy(y_ref.at[pl.ds(src, 1), :], buf_ref.at[sl, k, pl.ds(r, 1), :], sem.at[sl])

    def start_rows(step, sl):
        def body(r, carry):
            row_copy(step, sl, r, 0).start()
            row_copy(step, sl, r, 1).start()
            return carry
        lax.fori_loop(0, rows, body, 0, unroll=DMA_UNROLL)

    def wait_rows(step, sl):
        def body(r, carry):
            row_copy(step, sl, r, 0).wait()
            row_copy(step, sl, r, 1).wait()
            return carry
        lax.fori_loop(0, rows, body, 0, unroll=DMA_UNROLL)

    @pl.when(i == 0)
    def _():
        start_rows(i, slot)

    @pl.when(i + 1 < pl.num_programs(0))
    def _():
        start_rows(i + 1, 1 - slot)

    wait_rows(i, slot)
    w = wgt_ref[...]
    o_ref[...] = x_ref[...] + (w[:, 0:1] * buf_ref[slot, 0] + w[:, 1:2] * buf_ref[slot, 1])


def _combine(pos, x, wgt, y):
    t, d = x.shape
    blocks = [((ROW_TILE, d), F32), ((ROW_TILE, LANES), F32), ((ROW_TILE, d), F32)]
    grid_spec = pltpu.PrefetchScalarGridSpec(
        num_scalar_prefetch=1,
        grid=(t // ROW_TILE,),
        in_specs=[
            pl.BlockSpec((ROW_TILE, d), lambda i, pos: (i, 0)),
            pl.BlockSpec((ROW_TILE, LANES), lambda i, pos: (i, 0)),
            pl.BlockSpec(memory_space=pl.ANY),
        ],
        out_specs=pl.BlockSpec((ROW_TILE, d), lambda i, pos: (i, 0)),
        scratch_shapes=[pltpu.VMEM((2, 2, ROW_TILE, d), F32), pltpu.SemaphoreType.DMA((2,))],
    )
    return pl.pallas_call(
        _combine_kernel,
        name="moe_combine",
        grid_spec=grid_spec,
        out_shape=jax.ShapeDtypeStruct((t, d), F32),
        compiler_params=_params(("arbitrary",), blocks, scratch=_nbytes((2, 2, ROW_TILE, d), F32),
                                temps=2 * _nbytes((ROW_TILE, d), F32)),
    )(pos, x, wgt, y)


def _route_plan(idx, n_rows_pad):
    flat_e = idx.reshape(-1)
    onehot = (flat_e[:, None] == jnp.arange(N_EXPERTS, dtype=I32)[None, :]).astype(I32)
    csum = jnp.cumsum(onehot, axis=0)
    counts = csum[-1]
    rank = jnp.sum(onehot * (csum - 1), axis=1)
    tiles_per = (counts + MOE_TM - 1) // MOE_TM
    ends = jnp.cumsum(tiles_per)
    row_start = (ends - tiles_per) * MOE_TM
    pos = jnp.sum(onehot * row_start[None, :], axis=1) + rank
    n_tiles = ends[-1]
    tile_ids = jnp.arange(n_rows_pad // MOE_TM, dtype=I32)
    tile_expert = jnp.sum((tile_ids[:, None] >= ends[None, :]).astype(I32), axis=1)
    last_expert = jnp.sum((n_tiles - 1 >= ends).astype(I32))
    tile_expert = jnp.minimum(tile_expert, last_expert)
    return pos.astype(I32), tile_expert.astype(I32), n_tiles.reshape(1).astype(I32)


def _moe(layer, x, g, whi, wlo, wg, wu, wd):
    t, d = x.shape
    idx, wgt = _router(x, g, whi, wlo)
    n_pairs = 2 * t
    n_rows_pad = ((n_pairs + N_EXPERTS * (MOE_TM - 1)) // MOE_TM) * MOE_TM
    pos, tile_expert, n_tiles = _route_plan(idx[:, :2], n_rows_pad)
    xs = _dispatch(pos, x, g, jnp.zeros((n_rows_pad, d // 2), U32))
    gate_pre = _moe_proj(layer, tile_expert, n_tiles, xs, wg)
    act = _moe_proj(layer, tile_expert, n_tiles, xs, wu, gate_pre)
    y = _moe_down(layer, tile_expert, n_tiles, act, wd)
    return _combine(pos, x, wgt, y)


def _final_norm_kernel(x_ref, g_ref, o_ref):
    o_ref[...] = _rms(x_ref[...], g_ref[...])


def _final_norm(x, g, first_block, n_rows, rows):
    d = x.shape[1]
    blocks = [((rows, d), F32), ((rows, d), F32)]
    return pl.pallas_call(
        _final_norm_kernel,
        name="final_norm",
        grid=(n_rows // rows,),
        in_specs=[pl.BlockSpec((rows, d), lambda i: (first_block + i, 0)), pl.BlockSpec((1, d), lambda i: (0, 0))],
        out_specs=pl.BlockSpec((rows, d), lambda i: (i, 0)),
        out_shape=jax.ShapeDtypeStruct((n_rows, d), F32),
        compiler_params=_params(("parallel",), blocks, temps=2 * _nbytes((rows, d), F32)),
    )(x, g)


def _rotate_half_cols(w):
    half = w.shape[-1] // 2
    return jnp.concatenate([-w[..., half:], w[..., :half]], axis=-1)


def kernel(x_prompt, x_sample, mem_prompt, cache_kv_latent, cache_k_rope, cache_mem_k, cache_mem_v, page_table, norm_mix, w_in, q_norm, w_uq, kv_norm, w_uk, w_uv, chunk_v_norm, w_spatial, b_spatial, out_norm_mla, out_norm_chunk, w_out, norm_mem_q, norm_mem_kv, w_mem_q, w_mem_k, w_mem_v, w_mem_o, norm_ffn, w_gate_dense, w_up_dense, w_down_dense, w_router, w_gate_moe, w_up_moe, w_down_moe, final_norm):
    batch, seq, d = x_prompt.shape
    n_sample = x_sample.shape[0]
    depth = w_in.shape[0]
    mem_tokens = mem_prompt.shape[1]
    n_prompt = batch * seq
    t = n_prompt + n_sample
    past_len = page_table.shape[1] * cache_kv_latent.shape[2]
    mem_width = MEM_HEADS * MEM_DIM
    width = GROUPS * CHUNK

    o_kv, o_kr, o_u = LORA, 2 * LORA, 2 * LORA + ROPE
    w1 = jnp.concatenate([w_in[:, :, :o_kr], w_in[:, :, o_u:]], axis=2).astype(BF16)
    w_kr = w_in[:, :, o_kr:o_u]
    wkr2 = jnp.concatenate([w_kr, _rotate_half_cols(w_kr)], axis=2).astype(BF16)
    colgain = jnp.concatenate([q_norm, kv_norm, jnp.ones((depth, width), F32),
                               chunk_v_norm.reshape(depth, width)], axis=1).reshape(depth, 1, -1)
    q_rope_w = w_uq[..., NOPE:]
    wuq2 = jnp.concatenate([w_uq[..., :NOPE], q_rope_w, _rotate_half_cols(q_rope_w)], axis=-1)
    wuq2 = wuq2.transpose(0, 2, 1, 3).astype(BF16)
    wukt = w_uk.transpose(0, 2, 3, 1).astype(BF16)
    wuv_h = w_uv.transpose(0, 2, 1, 3).astype(BF16)
    wuv_all = w_uv.reshape(depth, LORA, HEADS * VDIM).astype(BF16)
    wmq = w_mem_q.reshape(depth, d, mem_width).astype(BF16)
    wmkv = jnp.concatenate([w_mem_k.reshape(depth, d, mem_width),
                            w_mem_v.reshape(depth, d, mem_width)], axis=2).astype(BF16)
    wmo = w_mem_o.reshape(depth, mem_width, d).astype(BF16)
    wr = jnp.pad(w_router, ((0, 0), (0, 0), (0, LANES - N_EXPERTS)))
    wr_hi = wr.astype(BF16)
    wr_lo = (wr - wr_hi.astype(F32)).astype(BF16)
    bst = b_spatial.transpose(0, 2, 1)
    mix_a = jnp.repeat(w_spatial[:, :, 0, 0], CHUNK, axis=1).reshape(depth, 1, width)
    mix_c = jnp.repeat(b_spatial[:, :, 0], CHUNK, axis=1).reshape(depth, 1, width)
    cache_k = cache_mem_k.reshape(depth, n_sample, mem_tokens * MEM_HEADS, MEM_DIM)
    cache_v = cache_mem_v.reshape(depth, n_sample, mem_tokens * MEM_HEADS, MEM_DIM)
    cache_krt = jnp.swapaxes(cache_k_rope, 2, 3)

    pos = jnp.concatenate([jnp.tile(jnp.arange(seq), batch), jnp.full((n_sample,), past_len)])
    inv_freq = 1.0 / (ROPE_THETA ** (jnp.arange(ROPE // 2, dtype=F32) / (ROPE // 2)))
    ang = pos.astype(F32)[:, None] * inv_freq[None, :]
    cos = jnp.tile(jnp.cos(ang), (1, 2))
    sin = jnp.tile(jnp.sin(ang), (1, 2))

    def gain(v):
        return v.reshape(1, -1)

    x = jnp.concatenate([x_prompt.reshape(n_prompt, d), x_sample.reshape(n_sample, d)], axis=0)
    mem_flat = mem_prompt.reshape(batch * mem_tokens, d)
    lat_p, kr_p, mk_p, mv_p, lat_s, kr_s, v_s = [], [], [], [], [], [], []
    for l in range(depth):
        z, kr, kcat = _in_proj(l, x, gain(norm_mix[l]), w1, wkr2, colgain, cos, sin)
        q = _q_proj(l, z, wuq2, wukt, cos, sin)
        o_att = _attn_prompt(l, q, kcat, wuv_h, batch, seq)
        qs = q[:, n_prompt:, :].transpose(1, 0, 2)
        ks = kcat[n_prompt:].reshape(n_sample, 1, KCAT)
        o_att = _attn_sample(l, page_table, qs, ks, wuv_all, o_att, cache_kv_latent, cache_krt)
        o_chk = _chunk_prompt(l, z, w_spatial, bst, n_prompt)
        o_chk = _chunk_sample(z, mix_a[l], mix_c[l], o_chk, n_sample)
        x = _out_proj(l, o_att, o_chk, gain(out_norm_mla[l]), gain(out_norm_chunk[l]), w_out, x)
        lat_p.append(z[:n_prompt, o_kv:o_kr].reshape(batch, seq, LORA))
        kr_p.append(kr[:n_prompt].reshape(batch, seq, ROPE))
        lat_s.append(z[n_prompt:, o_kv:o_kr].reshape(n_sample, 1, LORA))
        kr_s.append(kr[n_prompt:].reshape(n_sample, 1, ROPE))
        v_s.append(z[n_prompt:, 2 * LORA + width:].reshape(n_sample, 1, width))

        mkv = _norm_mm(l, mem_flat, gain(norm_mem_kv[l]), wmkv, 512, 512)
        mk_p.append(mkv[:, :mem_width].reshape(batch, mem_tokens, MEM_HEADS, MEM_DIM))
        mv_p.append(mkv[:, mem_width:].reshape(batch, mem_tokens, MEM_HEADS, MEM_DIM))
        x_new = _mem_prompt(l, x, gain(norm_mem_q[l]), wmq, mkv, wmo, n_prompt, seq, mem_tokens)
        x = _mem_sample(l, x, x_new, gain(norm_mem_q[l]), wmq, wmo, cache_k, cache_v, n_sample)

        i = l // 2
        if l % 2 == 0:
            act = _ffn_up(i, x, gain(norm_ffn[l]), w_gate_dense, w_up_dense)
            x = _ffn_down(i, act, w_down_dense, x)
        else:
            x = _moe(i, x, gain(norm_ffn[l]), wr_hi[i], wr_lo[i], w_gate_moe, w_up_moe, w_down_moe)

    y_prompt = _final_norm(x, gain(final_norm), 0, n_prompt, seq // 2)
    y_sample = _final_norm(x, gain(final_norm), n_prompt // n_sample, n_sample, n_sample)
    return (y_prompt.reshape(batch, seq, d), y_sample.reshape(n_sample, 1, d),
            jnp.stack(lat_p), jnp.stack(kr_p), jnp.stack(mk_p), jnp.stack(mv_p),
            jnp.stack(lat_s), jnp.stack(kr_s), jnp.stack(v_s))
```

```python
import functools

import jax
import jax.numpy as jnp
from jax import lax
from jax.experimental import pallas as pl
from jax.experimental.pallas import tpu as pltpu

F32 = jnp.float32
BF16 = jnp.bfloat16
U32 = jnp.uint32
I32 = jnp.int32

EPS = 1e-6
ROPE_THETA = 10000.0
ROPE = 64
NOPE = 128
LORA = 512
HEADS = 8
VDIM = 128
KCAT = 640
GROUPS = 8
CHUNK = 128
MEM_HEADS = 4
MEM_DIM = 128
N_EXPERTS = 8
MLA_SCALE = (NOPE + ROPE) ** -0.5
MEM_SCALE = MEM_DIM ** -0.5

LANES = 128
VMEM_CAP_BYTES = 60000 * 1024

TM = 1040
TQ = 256
SUB_PAGES = 16
MEM_SAMPLES_PER_STEP = 8
MOE_TM = 512
ROW_TILE = 208
DMA_UNROLL = 8


def _nbytes(shape, dtype):
    n = 1
    for s in shape:
        n *= s
    return n * jnp.dtype(dtype).itemsize


def _params(sem, blocks, scratch=0, temps=0):
    need = 2 * sum(_nbytes(s, d) for s, d in blocks) + scratch + temps
    return pltpu.CompilerParams(dimension_semantics=sem, vmem_limit_bytes=min(need, VMEM_CAP_BYTES))


def _rms(x, g):
    return x * lax.rsqrt(jnp.mean(x * x, axis=-1, keepdims=True) + EPS) * g


def _gelu(x):
    return x * (0.5 * (1.0 + jnp.tanh(0.7978845608028654 * (x + 0.044715 * (x * x * x)))))


def _sigmoid(x):
    return 1.0 / (1.0 + jnp.exp(-x))


def _dot(a, b):
    return jnp.dot(a, b, preferred_element_type=F32)


def _dot_nt(a, b):
    return lax.dot_general(a, b, (((1,), (1,)), ((), ())), preferred_element_type=F32)


def _in_proj_kernel(x_ref, g_ref, w_ref, wkr_ref, cg_ref, cos_ref, sin_ref,
                    z_ref, kr_ref, kcat_ref, hn_ref):
    j = pl.program_id(1)

    @pl.when(j == 0)
    def _():
        hn = _rms(x_ref[...], g_ref[...]).astype(BF16)
        hn_ref[...] = hn
        kr2 = _dot(hn, wkr_ref[...])
        kr = kr2[:, :ROPE] * cos_ref[...] + kr2[:, ROPE:] * sin_ref[...]
        kr_ref[...] = kr
        kcat_ref[:, LORA:KCAT] = jnp.concatenate([kr, jnp.zeros_like(kr)], axis=-1).astype(BF16)

    acc = _dot(hn_ref[...], w_ref[...])
    cg = cg_ref[...]

    @pl.when(j <= 1)
    def _():
        y = _rms(acc, cg)
        z_ref[...] = y

        @pl.when(j == 1)
        def _():
            kcat_ref[:, 0:LORA] = y.astype(BF16)

    @pl.when((j == 2) | (j == 3))
    def _():
        z_ref[...] = _gelu(acc)

    @pl.when(j >= 4)
    def _():
        gl = _gelu(acc)
        for k in range(LORA // CHUNK):
            sl = slice(k * CHUNK, (k + 1) * CHUNK)
            z_ref[:, sl] = _rms(gl[:, sl], cg[:, sl])


def _in_proj(layer, x, g, w1, wkr, colgain, cos, sin):
    t, d = x.shape
    n = w1.shape[2]
    tn = LORA
    blocks = [((TM, d), F32), ((d, tn), BF16), ((d, 2 * ROPE), BF16), ((TM, tn), F32),
              ((TM, LANES), F32), ((TM, KCAT), BF16), ((TM, 2 * LANES), F32)]
    return pl.pallas_call(
        _in_proj_kernel,
        name="in_proj",
        grid=(t // TM, n // tn),
        in_specs=[
            pl.BlockSpec((TM, d), lambda i, j: (i, 0)),
            pl.BlockSpec((1, d), lambda i, j: (0, 0)),
            pl.BlockSpec((None, d, tn), lambda i, j: (layer, 0, j)),
            pl.BlockSpec((None, d, 2 * ROPE), lambda i, j: (layer, 0, 0)),
            pl.BlockSpec((None, 1, tn), lambda i, j: (layer, 0, j)),
            pl.BlockSpec((TM, ROPE), lambda i, j: (i, 0)),
            pl.BlockSpec((TM, ROPE), lambda i, j: (i, 0)),
        ],
        out_specs=[
            pl.BlockSpec((TM, tn), lambda i, j: (i, j)),
            pl.BlockSpec((TM, ROPE), lambda i, j: (i, 0)),
            pl.BlockSpec((TM, KCAT), lambda i, j: (i, 0)),
        ],
        out_shape=[
            jax.ShapeDtypeStruct((t, n), F32),
            jax.ShapeDtypeStruct((t, ROPE), F32),
            jax.ShapeDtypeStruct((t, KCAT), BF16),
        ],
        scratch_shapes=[pltpu.VMEM((TM, d), BF16)],
        compiler_params=_params(("parallel", "arbitrary"), blocks,
                                scratch=_nbytes((TM, d), BF16), temps=3 * _nbytes((TM, d), F32)),
    )(x, g, w1, wkr, colgain, cos, sin)


def _q_proj_kernel(cq_ref, wuq_ref, wuk_ref, cos_ref, sin_ref, q_ref):
    cq = cq_ref[...].astype(BF16)
    q = _dot(cq, wuq_ref[...])
    qa = _dot(q[:, :NOPE].astype(BF16), wuk_ref[...])
    qr = q[:, NOPE:NOPE + ROPE] * cos_ref[...] + q[:, NOPE + ROPE:] * sin_ref[...]
    q_ref[:, 0:LORA] = (qa * MLA_SCALE).astype(BF16)
    q_ref[:, LORA:KCAT] = jnp.concatenate([qr * MLA_SCALE, jnp.zeros_like(qr)], axis=-1).astype(BF16)


def _q_proj(layer, z, wuq2, wukt, cos, sin):
    t = z.shape[0]
    blocks = [((TM, LORA), F32), ((LORA, 2 * NOPE), BF16), ((NOPE, LORA), BF16),
              ((TM, LANES), F32), ((TM, LANES), F32), ((TM, KCAT), BF16)]
    return pl.pallas_call(
        _q_proj_kernel,
        name="q_proj",
        grid=(t // TM, HEADS),
        in_specs=[
            pl.BlockSpec((TM, LORA), lambda i, h: (i, 0)),
            pl.BlockSpec((None, None, LORA, 2 * NOPE), lambda i, h: (layer, h, 0, 0)),
            pl.BlockSpec((None, None, NOPE, LORA), lambda i, h: (layer, h, 0, 0)),
            pl.BlockSpec((TM, ROPE), lambda i, h: (i, 0)),
            pl.BlockSpec((TM, ROPE), lambda i, h: (i, 0)),
        ],
        out_specs=pl.BlockSpec((None, TM, KCAT), lambda i, h: (h, i, 0)),
        out_shape=jax.ShapeDtypeStruct((HEADS, t, KCAT), BF16),
        compiler_params=_params(("parallel", "arbitrary"), blocks, temps=4 * _nbytes((TM, LORA), F32)),
    )(z, wuq2, wukt, cos, sin)


def _attn_prompt_kernel(q_ref, k_ref, wuv_ref, o_ref, m_ref, l_ref, acc_ref, *, nq):
    step = pl.program_id(0)
    n_steps = pl.num_programs(0) - 1
    qi = step % nq
    rows = HEADS * TQ

    @pl.when(step < n_steps)
    def _():
        q = q_ref[...].reshape(rows, KCAT)
        m_ref[...] = jnp.full((rows, 1), -jnp.inf, F32)
        l_ref[...] = jnp.zeros((rows, 1), F32)
        acc_ref[...] = jnp.zeros((rows, LORA), F32)

        def chunk(c):
            return k_ref[pl.ds(pl.multiple_of(c * TQ, TQ), TQ), :]

        def update(s, kc):
            m_prev = m_ref[...]
            m_new = jnp.maximum(m_prev, jnp.max(s, axis=-1, keepdims=True))
            alpha = jnp.exp(m_prev - m_new)
            p = jnp.exp(s - m_new)
            l_ref[...] = alpha * l_ref[...] + jnp.sum(p, axis=-1, keepdims=True)
            acc_ref[...] = alpha * acc_ref[...] + _dot(p.astype(BF16), kc[:, :LORA])
            m_ref[...] = m_new

        def body(c, s):
            s_next = _dot_nt(q, chunk(c + 1))
            update(s, chunk(c))
            return s_next

        s = lax.fori_loop(0, qi, body, _dot_nt(q, chunk(0)))
        q_pos = lax.broadcasted_iota(I32, (rows, TQ), 0) & (TQ - 1)
        k_pos = lax.broadcasted_iota(I32, (rows, TQ), 1)
        update(jnp.where(k_pos <= q_pos, s, -jnp.inf), chunk(qi))
        o = acc_ref[...] * (1.0 / l_ref[...])
        for h in range(HEADS):
            oh = o[h * TQ:(h + 1) * TQ, :].astype(BF16)
            o_ref[:, h * VDIM:(h + 1) * VDIM] = _dot(oh, wuv_ref[h])

    @pl.when(step == n_steps)
    def _():
        o_ref[...] = jnp.zeros(o_ref.shape, F32)


def _attn_prompt(layer, q, kcat, wuv_h, batch, seq):
    t = kcat.shape[0]
    nq = seq // TQ
    n_steps = batch * nq
    rows = HEADS * TQ
    blocks = [((HEADS, TQ, KCAT), BF16), ((seq, KCAT), BF16), ((HEADS, LORA, VDIM), BF16),
              ((TQ, HEADS * VDIM), F32)]
    scratch = 2 * _nbytes((rows, LANES), F32) + _nbytes((rows, LORA), F32)
    return pl.pallas_call(
        functools.partial(_attn_prompt_kernel, nq=nq),
        name="attn_prompt",
        grid=(n_steps + 1,),
        in_specs=[
            pl.BlockSpec((HEADS, TQ, KCAT), lambda i: (0, jnp.minimum(i, n_steps - 1), 0)),
            pl.BlockSpec((seq, KCAT), lambda i: (jnp.minimum(i, n_steps - 1) // nq, 0)),
            pl.BlockSpec((None, HEADS, LORA, VDIM), lambda i: (layer, 0, 0, 0)),
        ],
        out_specs=pl.BlockSpec((TQ, HEADS * VDIM), lambda i: (i, 0)),
        out_shape=jax.ShapeDtypeStruct((t, HEADS * VDIM), F32),
        scratch_shapes=[pltpu.VMEM((rows, 1), F32), pltpu.VMEM((rows, 1), F32),
                        pltpu.VMEM((rows, LORA), F32)],
        compiler_params=_params(("arbitrary",), blocks, scratch=scratch,
                                temps=6 * _nbytes((rows, TQ), F32) + 2 * _nbytes((rows, LORA), F32)),
    )(q, kcat, wuv_h)


def _attn_sample_kernel(pt_ref, q_ref, ks_ref, wuv_ref, lat_hbm, krt_hbm, oin_ref, o_ref,
                        latbuf_ref, krtbuf_ref, sem, *, layer, n_pages):
    del oin_ref
    b = pl.program_id(0)
    slot = b % 2
    page = lat_hbm.shape[2]
    keys = SUB_PAGES * page

    def page_copies(sample, dst_slot):
        copies = []
        for g in range(n_pages):
            pg = pt_ref[sample * n_pages + g]
            span = pl.ds(g * page, page)
            copies.append(pltpu.make_async_copy(lat_hbm.at[layer, pg], latbuf_ref.at[dst_slot, span, :],
                                                sem.at[0, dst_slot]))
            copies.append(pltpu.make_async_copy(krt_hbm.at[layer, pg], krtbuf_ref.at[dst_slot, :, span],
                                                sem.at[1, dst_slot]))
        return copies

    @pl.when(b == 0)
    def _():
        for c in page_copies(b, slot):
            c.start()

    @pl.when(b + 1 < pl.num_programs(0))
    def _():
        for c in page_copies(b + 1, 1 - slot):
            c.start()

    for c in page_copies(b, slot):
        c.wait()

    q = q_ref[0]
    qa = jnp.concatenate([q[:, :LORA], jnp.zeros((LANES - HEADS, LORA), BF16)], axis=0)
    qr = q[:, LORA:LORA + ROPE]
    n_sub = n_pages // SUB_PAGES
    lats = [latbuf_ref[slot, j * keys:(j + 1) * keys, :].astype(BF16) for j in range(n_sub)]
    s_ts = [_dot_nt(lats[j], qa) for j in range(n_sub)]
    parts = []
    for j in range(n_sub):
        krt = krtbuf_ref[slot, :, j * keys:(j + 1) * keys].astype(BF16)
        s = s_ts[j].T[:HEADS, :] + _dot(qr, krt)
        m_j = jnp.max(s, axis=-1, keepdims=True)
        p = jnp.exp(s - m_j)
        parts.append((m_j, jnp.sum(p, axis=-1, keepdims=True), _dot(p.astype(BF16), lats[j])))

    q = q.astype(F32)

    k_self = ks_ref[0].astype(F32)
    s_self = jnp.sum(q * k_self, axis=-1, keepdims=True)
    m = s_self
    for m_j, _, _ in parts:
        m = jnp.maximum(m, m_j)
    w_self = jnp.exp(s_self - m)
    l = w_self
    acc = w_self * k_self[:, :LORA]
    for m_j, l_j, acc_j in parts:
        a = jnp.exp(m_j - m)
        l = l + a * l_j
        acc = acc + a * acc_j
    o_lat = acc * (1.0 / l)
    r = _dot(o_lat.astype(BF16), wuv_ref[...])
    head = lax.broadcasted_iota(I32, r.shape, 0)
    col_head = lax.broadcasted_iota(I32, r.shape, 1) // VDIM
    o_ref[pl.ds(b, 1), :] = jnp.sum(jnp.where(head == col_head, r, 0.0), axis=0, keepdims=True)


def _attn_sample(layer, page_table, qs, ks, wuv_all, o_att, cache_lat, cache_krt):
    nb, n_pages = page_table.shape
    page = cache_lat.shape[2]
    past = n_pages * page
    t = o_att.shape[0]
    row_block = (t - nb) // nb
    pt_flat = page_table.reshape(-1)
    keys = SUB_PAGES * page
    blocks = [((HEADS, KCAT), BF16), ((1, KCAT), BF16), ((LORA, HEADS * VDIM), BF16),
              ((nb, HEADS * VDIM), F32)]
    scratch = _nbytes((2, past, LORA), F32) + _nbytes((2, ROPE, past), F32)
    grid_spec = pltpu.PrefetchScalarGridSpec(
        num_scalar_prefetch=1,
        grid=(nb,),
        in_specs=[
            pl.BlockSpec((1, HEADS, KCAT), lambda b, pt: (b, 0, 0)),
            pl.BlockSpec((1, 1, KCAT), lambda b, pt: (b, 0, 0)),
            pl.BlockSpec((None, LORA, HEADS * VDIM), lambda b, pt: (layer, 0, 0)),
            pl.BlockSpec(memory_space=pl.ANY),
            pl.BlockSpec(memory_space=pl.ANY),
            pl.BlockSpec(memory_space=pl.ANY),
        ],
        out_specs=pl.BlockSpec((nb, HEADS * VDIM), lambda b, pt: (row_block, 0)),
        scratch_shapes=[pltpu.VMEM((2, past, LORA), F32), pltpu.VMEM((2, ROPE, past), F32),
                        pltpu.SemaphoreType.DMA((2, 2))],
    )
    return pl.pallas_call(
        functools.partial(_attn_sample_kernel, layer=layer, n_pages=n_pages),
        name="attn_sample",
        grid_spec=grid_spec,
        out_shape=jax.ShapeDtypeStruct(o_att.shape, F32),
        input_output_aliases={6: 0},
        compiler_params=_params(("arbitrary",), blocks, scratch=scratch,
                                temps=(n_pages // SUB_PAGES + 2) * (_nbytes((keys, LANES), F32)
                                                                    + _nbytes((keys, LORA), BF16))),
    )(pt_flat, qs, ks, wuv_all, cache_lat, cache_krt, o_att)


def _chunk_prompt_kernel(u_ref, v_ref, ws_ref, bst_ref, o_ref):
    step = pl.program_id(0)
    n_steps = pl.num_programs(0) - 1

    @pl.when(step < n_steps)
    def _():
        row = lax.broadcasted_iota(I32, (CHUNK, CHUNK), 0)
        col = lax.broadcasted_iota(I32, (CHUNK, CHUNK), 1)
        n_chunks = u_ref.shape[0] // CHUNK
        for g in range(GROUPS):
            w = jnp.where(row >= col, ws_ref[g], 0.0).astype(BF16)
            bias = bst_ref[:, g:g + 1]
            cs = slice(g * CHUNK, (g + 1) * CHUNK)
            for n in range(n_chunks):
                rs = slice(n * CHUNK, (n + 1) * CHUNK)
                mixed = _dot(w, v_ref[rs, cs].astype(BF16)) + bias
                o_ref[rs, cs] = u_ref[rs, cs] * mixed

    @pl.when(step == n_steps)
    def _():
        o_ref[...] = jnp.zeros(o_ref.shape, F32)


def _chunk_prompt(layer, z, ws, bst, n_prompt):
    t = z.shape[0]
    width = GROUPS * CHUNK
    rows = 4 * CHUNK
    n_steps = n_prompt // rows
    blocks = [((rows, width), F32)] * 3 + [((GROUPS, CHUNK, CHUNK), F32), ((CHUNK, LANES), F32)]
    return pl.pallas_call(
        _chunk_prompt_kernel,
        name="chunk_prompt",
        grid=(n_steps + 1,),
        in_specs=[
            pl.BlockSpec((rows, width), lambda i: (jnp.minimum(i, n_steps - 1), 1)),
            pl.BlockSpec((rows, width), lambda i: (jnp.minimum(i, n_steps - 1), 2)),
            pl.BlockSpec((None, GROUPS, CHUNK, CHUNK), lambda i: (layer, 0, 0, 0)),
            pl.BlockSpec((None, CHUNK, GROUPS), lambda i: (layer, 0, 0)),
        ],
        out_specs=pl.BlockSpec((rows, width), lambda i: (i, 0)),
        out_shape=jax.ShapeDtypeStruct((t, width), F32),
        compiler_params=_params(("arbitrary",), blocks, temps=_nbytes((rows, width), F32)),
    )(z, z, ws, bst)


def _chunk_sample_kernel(u_ref, v_ref, a_ref, c_ref, oin_ref, o_ref):
    del oin_ref
    o_ref[...] = u_ref[...] * (a_ref[...] * v_ref[...] + c_ref[...])


def _chunk_sample(z, a, c, o_chk, n_sample):
    t = z.shape[0]
    width = GROUPS * CHUNK
    rb = (t - n_sample) // n_sample
    blocks = [((n_sample, width), F32)] * 3 + [((8, width), F32)] * 2
    return pl.pallas_call(
        _chunk_sample_kernel,
        name="chunk_sample",
        grid=(1,),
        in_specs=[
            pl.BlockSpec((n_sample, width), lambda i: (rb, 1)),
            pl.BlockSpec((n_sample, width), lambda i: (rb, 2)),
            pl.BlockSpec((1, width), lambda i: (0, 0)),
            pl.BlockSpec((1, width), lambda i: (0, 0)),
            pl.BlockSpec(memory_space=pl.ANY),
        ],
        out_specs=pl.BlockSpec((n_sample, width), lambda i: (rb, 0)),
        out_shape=jax.ShapeDtypeStruct(o_chk.shape, F32),
        input_output_aliases={4: 0},
        compiler_params=_params(("arbitrary",), blocks),
    )(z, z, a, c, o_chk)


def _out_proj_kernel(oa_ref, oc_ref, ga_ref, gc_ref, w_ref, x_ref, o_ref, cat_ref):
    half = oa_ref.shape[1]

    @pl.when(pl.program_id(1) == 0)
    def _():
        cat_ref[:, 0:half] = _rms(oa_ref[...], ga_ref[...]).astype(BF16)
        cat_ref[:, half:] = _rms(oc_ref[...], gc_ref[...]).astype(BF16)

    o_ref[...] = x_ref[...] + _dot(cat_ref[...], w_ref[...].astype(BF16))


def _out_proj(layer, o_att, o_chk, ga, gc, w, x):
    t, d = x.shape
    half = o_att.shape[1]
    tn = 512
    blocks = [((TM, half), F32)] * 2 + [((d, tn), F32), ((TM, tn), F32), ((TM, tn), F32)]
    return pl.pallas_call(
        _out_proj_kernel,
        name="out_proj",
        grid=(t // TM, d // tn),
        in_specs=[
            pl.BlockSpec((TM, half), lambda i, j: (i, 0)),
            pl.BlockSpec((TM, half), lambda i, j: (i, 0)),
            pl.BlockSpec((1, half), lambda i, j: (0, 0)),
            pl.BlockSpec((1, half), lambda i, j: (0, 0)),
            pl.BlockSpec((None, d, tn), lambda i, j: (layer, 0, j)),
            pl.BlockSpec((TM, tn), lambda i, j: (i, j)),
        ],
        out_specs=pl.BlockSpec((TM, tn), lambda i, j: (i, j)),
        out_shape=jax.ShapeDtypeStruct((t, d), F32),
        scratch_shapes=[pltpu.VMEM((TM, d), BF16)],
        compiler_params=_params(("parallel", "arbitrary"), blocks, scratch=_nbytes((TM, d), BF16),
                                temps=3 * _nbytes((TM, half), F32)),
    )(o_att, o_chk, ga, gc, w, x)


def _norm_mm_kernel(x_ref, g_ref, w_ref, o_ref, hn_ref):
    @pl.when(pl.program_id(1) == 0)
    def _():
        hn_ref[...] = _rms(x_ref[...], g_ref[...]).astype(BF16)

    o_ref[...] = _dot(hn_ref[...], w_ref[...])


def _norm_mm(layer, x, g, w, tm, tn):
    m, d = x.shape
    n = w.shape[2]
    blocks = [((tm, d), F32), ((d, tn), BF16), ((tm, tn), F32)]
    return pl.pallas_call(
        _norm_mm_kernel,
        name="mem_kv",
        grid=(m // tm, n // tn),
        in_specs=[
            pl.BlockSpec((tm, d), lambda i, j: (i, 0)),
            pl.BlockSpec((1, d), lambda i, j: (0, 0)),
            pl.BlockSpec((None, d, tn), lambda i, j: (layer, 0, j)),
        ],
        out_specs=pl.BlockSpec((tm, tn), lambda i, j: (i, j)),
        out_shape=jax.ShapeDtypeStruct((m, n), F32),
        scratch_shapes=[pltpu.VMEM((tm, d), BF16)],
        compiler_params=_params(("parallel", "arbitrary"), blocks, scratch=_nbytes((tm, d), BF16),
                                temps=3 * _nbytes((tm, d), F32)),
    )(x, g, w)


def _mem_prompt_kernel(x_ref, g_ref, wq_ref, kv_ref, wo_ref, o_ref):
    step = pl.program_id(0)
    n_steps = pl.num_programs(0) - 1

    @pl.when(step < n_steps)
    def _():
        x = x_ref[...]
        q = _dot(_rms(x, g_ref[...]).astype(BF16), wq_ref[...])
        width = MEM_HEADS * MEM_DIM
        outs = []
        for h in range(MEM_HEADS):
            hs = slice(h * MEM_DIM, (h + 1) * MEM_DIM)
            vs = slice(width + h * MEM_DIM, width + (h + 1) * MEM_DIM)
            s = _dot_nt(q[:, hs].astype(BF16), kv_ref[:, hs].astype(BF16)) * MEM_SCALE
            p = jnp.exp(s - jnp.max(s, axis=-1, keepdims=True))
            p = p * (1.0 / jnp.sum(p, axis=-1, keepdims=True))
            outs.append(_dot(p.astype(BF16), kv_ref[:, vs].astype(BF16)))
        o = jnp.concatenate(outs, axis=-1).astype(BF16)
        o_ref[...] = x + _dot(o, wo_ref[...])

    @pl.when(step == n_steps)
    def _():
        o_ref[...] = jnp.zeros(o_ref.shape, F32)


def _mem_prompt(layer, x, g, wq, mkv, wo, n_prompt, seq, mem_tokens):
    t, d = x.shape
    rows = 512
    width = MEM_HEADS * MEM_DIM
    per_batch = seq // rows
    n_steps = n_prompt // rows
    blocks = [((rows, d), F32), ((d, width), BF16), ((mem_tokens, 2 * width), F32),
              ((width, d), BF16), ((rows, d), F32)]
    return pl.pallas_call(
        _mem_prompt_kernel,
        name="mem_prompt",
        grid=(n_steps + 1,),
        in_specs=[
            pl.BlockSpec((rows, d), lambda i: (jnp.minimum(i, n_steps - 1), 0)),
            pl.BlockSpec((1, d), lambda i: (0, 0)),
            pl.BlockSpec((None, d, width), lambda i: (layer, 0, 0)),
            pl.BlockSpec((mem_tokens, 2 * width), lambda i: (jnp.minimum(i, n_steps - 1) // per_batch, 0)),
            pl.BlockSpec((None, width, d), lambda i: (layer, 0, 0)),
        ],
        out_specs=pl.BlockSpec((rows, d), lambda i: (i, 0)),
        out_shape=jax.ShapeDtypeStruct((t, d), F32),
        compiler_params=_params(("arbitrary",), blocks, temps=4 * _nbytes((rows, d), F32)),
    )(x, g, wq, mkv, wo)


def _mem_sample_kernel(x_ref, g_ref, wq_ref, k_ref, v_ref, wo_ref, xin_ref, o_ref, q_scr, o_scr):
    del xin_ref
    step = pl.program_id(0)
    width = MEM_HEADS * MEM_DIM

    @pl.when(step == 0)
    def _():
        q_scr[...] = _dot(_rms(x_ref[...], g_ref[...]).astype(BF16), wq_ref[...])

    kv_rows = k_ref.shape[1]
    row_head = lax.broadcasted_iota(I32, (8, kv_rows), 1) & (MEM_HEADS - 1)
    sub = lax.broadcasted_iota(I32, (8, kv_rows), 0)
    own = (row_head == sub) | (sub >= MEM_HEADS)
    pad = jnp.zeros((8 - MEM_HEADS, MEM_DIM), F32)
    for n in range(MEM_SAMPLES_PER_STEP):
        r = step * MEM_SAMPLES_PER_STEP + n
        q_row = q_scr[pl.ds(r, 1), :]
        q_heads = jnp.concatenate([q_row[:, h * MEM_DIM:(h + 1) * MEM_DIM] for h in range(MEM_HEADS)]
                                  + [pad], axis=0)
        s = _dot_nt(q_heads.astype(BF16), k_ref[n].astype(BF16)) * MEM_SCALE
        s = jnp.where(own, s, -jnp.inf)
        p = jnp.exp(s - jnp.max(s, axis=-1, keepdims=True))
        p = p * (1.0 / jnp.sum(p, axis=-1, keepdims=True))
        ob = _dot(p.astype(BF16), v_ref[n].astype(BF16))
        o_scr[pl.ds(r, 1), :] = jnp.concatenate([ob[h:h + 1, :] for h in range(MEM_HEADS)], axis=1)

    @pl.when(step == pl.num_programs(0) - 1)
    def _():
        o_ref[...] = x_ref[...] + _dot(o_scr[...].astype(BF16), wo_ref[...])


def _mem_sample(layer, x_mid, x_new, g, wq, wo, cache_k, cache_v, n_sample):
    t, d = x_mid.shape
    width = MEM_HEADS * MEM_DIM
    kv_rows = cache_k.shape[2]
    per = MEM_SAMPLES_PER_STEP
    rb = (t - n_sample) // n_sample
    blocks = [((n_sample, d), F32), ((d, width), BF16), ((per, kv_rows, MEM_DIM), F32),
              ((per, kv_rows, MEM_DIM), F32), ((width, d), BF16), ((n_sample, d), F32)]
    return pl.pallas_call(
        _mem_sample_kernel,
        name="mem_sample",
        grid=(n_sample // per,),
        in_specs=[
            pl.BlockSpec((n_sample, d), lambda i: (rb, 0)),
            pl.BlockSpec((1, d), lambda i: (0, 0)),
            pl.BlockSpec((None, d, width), lambda i: (layer, 0, 0)),
            pl.BlockSpec((None, per, kv_rows, MEM_DIM), lambda i: (layer, i, 0, 0)),
            pl.BlockSpec((None, per, kv_rows, MEM_DIM), lambda i: (layer, i, 0, 0)),
            pl.BlockSpec((None, width, d), lambda i: (layer, 0, 0)),
            pl.BlockSpec(memory_space=pl.ANY),
        ],
        out_specs=pl.BlockSpec((n_sample, d), lambda i: (rb, 0)),
        out_shape=jax.ShapeDtypeStruct((t, d), F32),
        scratch_shapes=[pltpu.VMEM((n_sample, width), F32), pltpu.VMEM((n_sample, width), F32)],
        input_output_aliases={6: 0},
        compiler_params=_params(("arbitrary",), blocks, scratch=2 * _nbytes((n_sample, width), F32),
                                temps=4 * _nbytes((n_sample, d), F32)),
    )(x_mid, g, wq, cache_k, cache_v, wo, x_new)


def _ffn_up_kernel(x_ref, g_ref, wg_ref, wu_ref, o_ref, hn_ref):
    @pl.when(pl.program_id(1) == 0)
    def _():
        hn_ref[...] = _rms(x_ref[...], g_ref[...]).astype(BF16)

    hn = hn_ref[...]
    a = _dot(hn, wg_ref[...].astype(BF16))
    b = _dot(hn, wu_ref[...].astype(BF16))
    o_ref[...] = (a * _sigmoid(a) * b).astype(BF16)


def _ffn_up(layer, x, g, wg, wu):
    t, d = x.shape
    f = wg.shape[2]
    tn = 512
    blocks = [((TM, d), F32), ((d, tn), F32), ((d, tn), F32), ((TM, tn), BF16)]
    return pl.pallas_call(
        _ffn_up_kernel,
        name="ffn_up",
        grid=(t // TM, f // tn),
        in_specs=[
            pl.BlockSpec((TM, d), lambda i, j: (i, 0)),
            pl.BlockSpec((1, d), lambda i, j: (0, 0)),
            pl.BlockSpec((None, d, tn), lambda i, j: (layer, 0, j)),
            pl.BlockSpec((None, d, tn), lambda i, j: (layer, 0, j)),
        ],
        out_specs=pl.BlockSpec((TM, tn), lambda i, j: (i, j)),
        out_shape=jax.ShapeDtypeStruct((t, f), BF16),
        scratch_shapes=[pltpu.VMEM((TM, d), BF16)],
        compiler_params=_params(("parallel", "arbitrary"), blocks, scratch=_nbytes((TM, d), BF16),
                                temps=3 * _nbytes((TM, d), F32)),
    )(x, g, wg, wu)


def _mm_res_kernel(a_ref, w_ref, x_ref, o_ref, wb_ref):
    @pl.when(pl.program_id(1) == 0)
    def _():
        wb_ref[...] = w_ref[...].astype(BF16)

    o_ref[...] = x_ref[...] + _dot(a_ref[...], wb_ref[...])


def _ffn_down(layer, act, wd, x):
    t, f = act.shape
    d = wd.shape[2]
    tm, tn = 640, 512
    blocks = [((tm, f), BF16), ((f, tn), F32), ((tm, tn), F32), ((tm, tn), F32)]
    return pl.pallas_call(
        _mm_res_kernel,
        name="ffn_down",
        grid=(d // tn, t // tm),
        in_specs=[
            pl.BlockSpec((tm, f), lambda j, i: (i, 0)),
            pl.BlockSpec((None, f, tn), lambda j, i: (layer, 0, j)),
            pl.BlockSpec((tm, tn), lambda j, i: (i, j)),
        ],
        out_specs=pl.BlockSpec((tm, tn), lambda j, i: (i, j)),
        out_shape=jax.ShapeDtypeStruct((t, d), F32),
        scratch_shapes=[pltpu.VMEM((f, tn), BF16)],
        compiler_params=_params(("parallel", "arbitrary"), blocks, scratch=_nbytes((f, tn), BF16),
                                temps=2 * _nbytes((tm, tn), F32)),
    )(act, wd, x)


def _router_kernel(x_ref, g_ref, whi_ref, wlo_ref, idx_ref, wgt_ref):
    h = _rms(x_ref[...], g_ref[...])
    h_hi = h.astype(BF16)
    h_lo = (h - h_hi.astype(F32)).astype(BF16)
    logits = _dot(h_hi, whi_ref[...]) + (_dot(h_lo, whi_ref[...]) + _dot(h_hi, wlo_ref[...]))
    lane = lax.broadcasted_iota(I32, logits.shape, 1)
    logits = jnp.where(lane < N_EXPERTS, logits, -jnp.inf)
    v1 = jnp.max(logits, axis=-1, keepdims=True)
    i1 = jnp.min(jnp.where(logits == v1, lane, LANES), axis=-1, keepdims=True)
    rest = jnp.where(lane == i1, -jnp.inf, logits)
    v2 = jnp.max(rest, axis=-1, keepdims=True)
    i2 = jnp.min(jnp.where(rest == v2, lane, LANES), axis=-1, keepdims=True)
    e2 = jnp.exp(v2 - v1)
    w1 = 1.0 / (1.0 + e2)
    w2 = e2 * w1
    idx_ref[...] = jnp.where(lane == 0, i1, jnp.where(lane == 1, i2, 0))
    wgt_ref[...] = jnp.where(lane == 0, w1, jnp.where(lane == 1, w2, 0.0))


def _router(x, g, whi, wlo):
    t, d = x.shape
    blocks = [((TM, d), F32), ((d, LANES), BF16), ((d, LANES), BF16), ((TM, LANES), I32), ((TM, LANES), F32)]
    return pl.pallas_call(
        _router_kernel,
        name="router",
        grid=(t // TM,),
        in_specs=[
            pl.BlockSpec((TM, d), lambda i: (i, 0)),
            pl.BlockSpec((1, d), lambda i: (0, 0)),
            pl.BlockSpec((d, LANES), lambda i: (0, 0)),
            pl.BlockSpec((d, LANES), lambda i: (0, 0)),
        ],
        out_specs=[pl.BlockSpec((TM, LANES), lambda i: (i, 0)), pl.BlockSpec((TM, LANES), lambda i: (i, 0))],
        out_shape=[jax.ShapeDtypeStruct((t, LANES), I32), jax.ShapeDtypeStruct((t, LANES), F32)],
        compiler_params=_params(("parallel",), blocks, temps=5 * _nbytes((TM, d), F32)),
    )(x, g, whi, wlo)


def _dispatch_kernel(pos_ref, x_ref, g_ref, xs_in_ref, xs_ref, buf_ref, sem):
    del xs_in_ref
    i = pl.program_id(0)
    rows, d = x_ref.shape
    half = d // 2
    slot = i % 2
    h = _rms(x_ref[...], g_ref[...]).astype(BF16).astype(F32)
    lo = pltpu.bitcast(h[:, :half], U32) >> jnp.uint32(16)
    hi = pltpu.bitcast(h[:, half:], U32)
    buf_ref[slot] = lo | hi

    def row_copy(step, sl, r, k):
        dst = pos_ref[2 * (step * rows + r) + k]
        return pltpu.make_async_copy(buf_ref.at[sl, pl.ds(r, 1), :], xs_ref.at[pl.ds(dst, 1), :], sem.at[sl])

    def start_rows(step, sl):
        def body(r, carry):
            row_copy(step, sl, r, 0).start()
            row_copy(step, sl, r, 1).start()
            return carry
        lax.fori_loop(0, rows, body, 0, unroll=DMA_UNROLL)

    def wait_rows(step, sl):
        def body(r, carry):
            row_copy(step, sl, r, 0).wait()
            row_copy(step, sl, r, 1).wait()
            return carry
        lax.fori_loop(0, rows, body, 0, unroll=DMA_UNROLL)

    start_rows(i, slot)

    @pl.when(i > 0)
    def _():
        wait_rows(i - 1, 1 - slot)

    @pl.when(i == pl.num_programs(0) - 1)
    def _():
        wait_rows(i, slot)


def _dispatch(pos, x, g, xs_init):
    t, d = x.shape
    blocks = [((ROW_TILE, d), F32)]
    grid_spec = pltpu.PrefetchScalarGridSpec(
        num_scalar_prefetch=1,
        grid=(t // ROW_TILE,),
        in_specs=[
            pl.BlockSpec((ROW_TILE, d), lambda i, pos: (i, 0)),
            pl.BlockSpec((1, d), lambda i, pos: (0, 0)),
            pl.BlockSpec(memory_space=pl.ANY),
        ],
        out_specs=pl.BlockSpec(memory_space=pl.ANY),
        scratch_shapes=[pltpu.VMEM((2, ROW_TILE, d // 2), U32), pltpu.SemaphoreType.DMA((2,))],
    )
    return pl.pallas_call(
        _dispatch_kernel,
        name="moe_dispatch",
        grid_spec=grid_spec,
        out_shape=jax.ShapeDtypeStruct(xs_init.shape, U32),
        input_output_aliases={3: 0},
        compiler_params=_params(("arbitrary",), blocks, scratch=_nbytes((2, ROW_TILE, d // 2), U32),
                                temps=4 * _nbytes((ROW_TILE, d), F32)),
    )(pos, x, g, xs_init)


def _unpack_pairs(u):
    lo = pltpu.bitcast(u << jnp.uint32(16), F32).astype(BF16)
    hi = pltpu.bitcast(u & jnp.uint32(0xFFFF0000), F32).astype(BF16)
    return lo, hi


def _new_expert(te_ref, i):
    return (i == 0) | (te_ref[i] != te_ref[jnp.maximum(i - 1, 0)])


def _moe_proj_kernel(te_ref, nt_ref, x_ref, w_ref, *rest, gated):
    o_ref, wb_ref = rest[-2:]
    i = pl.program_id(1)
    half = x_ref.shape[1]

    @pl.when(_new_expert(te_ref, i))
    def _():
        wb_ref[...] = w_ref[...].astype(BF16)

    @pl.when(i < nt_ref[0])
    def _():
        lo, hi = _unpack_pairs(x_ref[...])
        y = _dot(lo, wb_ref[0:half, :]) + _dot(hi, wb_ref[half:, :])
        if gated:
            a = rest[0][...]
            o_ref[...] = (a * _sigmoid(a) * y).astype(o_ref.dtype)
        else:
            o_ref[...] = y

    @pl.when(i >= nt_ref[0])
    def _():
        o_ref[...] = jnp.zeros(o_ref.shape, o_ref.dtype)


def _moe_proj(layer, tile_expert, n_tiles, xs, w, gate_pre=None):
    r, half = xs.shape
    d, f = w.shape[2], w.shape[3]
    tf = f // 2
    gated = gate_pre is not None
    out_dtype = BF16 if gated else F32
    blocks = [((MOE_TM, half), U32), ((d, tf), F32), ((MOE_TM, tf), F32), ((MOE_TM, tf), out_dtype)]
    in_specs = [
        pl.BlockSpec((MOE_TM, half), lambda j, i, te, nt: (i, 0)),
        pl.BlockSpec((None, None, d, tf), lambda j, i, te, nt: (layer, te[i], 0, j)),
    ]
    operands = [tile_expert, n_tiles, xs, w]
    if gated:
        in_specs.append(pl.BlockSpec((MOE_TM, tf), lambda j, i, te, nt: (i, j)))
        operands.append(gate_pre)
    grid_spec = pltpu.PrefetchScalarGridSpec(
        num_scalar_prefetch=2,
        grid=(f // tf, r // MOE_TM),
        in_specs=in_specs,
        out_specs=pl.BlockSpec((MOE_TM, tf), lambda j, i, te, nt: (i, j)),
        scratch_shapes=[pltpu.VMEM((d, tf), BF16)],
    )
    return pl.pallas_call(
        functools.partial(_moe_proj_kernel, gated=gated),
        name="moe_up" if gated else "moe_gate",
        grid_spec=grid_spec,
        out_shape=jax.ShapeDtypeStruct((r, f), out_dtype),
        compiler_params=_params(("arbitrary", "arbitrary"), blocks, scratch=_nbytes((d, tf), BF16),
                                temps=3 * _nbytes((MOE_TM, tf), F32)),
    )(*operands)


def _moe_down_kernel(te_ref, nt_ref, a_ref, w_ref, o_ref, wb_ref):
    i = pl.program_id(1)

    @pl.when(_new_expert(te_ref, i))
    def _():
        wb_ref[...] = w_ref[...].astype(BF16)

    @pl.when(i < nt_ref[0])
    def _():
        o_ref[...] = _dot(a_ref[...], wb_ref[...])

    @pl.when(i >= nt_ref[0])
    def _():
        o_ref[...] = jnp.zeros(o_ref.shape, F32)


def _moe_down(layer, tile_expert, n_tiles, act, wd):
    r, f = act.shape
    d = wd.shape[3]
    tn = d // 2
    blocks = [((MOE_TM, f), BF16), ((f, tn), F32), ((MOE_TM, tn), F32)]
    grid_spec = pltpu.PrefetchScalarGridSpec(
        num_scalar_prefetch=2,
        grid=(d // tn, r // MOE_TM),
        in_specs=[
            pl.BlockSpec((MOE_TM, f), lambda j, i, te, nt: (i, 0)),
            pl.BlockSpec((None, None, f, tn), lambda j, i, te, nt: (layer, te[i], 0, j)),
        ],
        out_specs=pl.BlockSpec((MOE_TM, tn), lambda j, i, te, nt: (i, j)),
        scratch_shapes=[pltpu.VMEM((f, tn), BF16)],
    )
    return pl.pallas_call(
        _moe_down_kernel,
        name="moe_down",
        grid_spec=grid_spec,
        out_shape=jax.ShapeDtypeStruct((r, d), F32),
        compiler_params=_params(("arbitrary", "arbitrary"), blocks, scratch=_nbytes((f, tn), BF16),
                                temps=2 * _nbytes((MOE_TM, tn), F32)),
    )(tile_expert, n_tiles, act, wd)


def _combine_kernel(pos_ref, x_ref, wgt_ref, y_ref, o_ref, buf_ref, sem):
    i = pl.program_id(0)
    rows = x_ref.shape[0]
    slot = i % 2

    def row_copy(step, sl, r, k):
        src = pos_ref[2 * (step * rows + r) + k]
        return pltpu.make_async_copy(y_ref.at[pl.ds(src, 1), :], buf_ref.at[sl, k, pl.ds(r, 1), :], sem.at[sl])

    def start_rows(step, sl):
        def body(r, carry):
            row_copy(step, sl, r, 0).start()
            row_copy(step, sl, r, 1).start()
            return carry
        lax.fori_loop(0, rows, body, 0, unroll=DMA_UNROLL)

    def wait_rows(step, sl):
        def body(r, carry):
            row_copy(step, sl, r, 0).wait()
            row_copy(step, sl, r, 1).wait()
            return carry
        lax.fori_loop(0, rows, body, 0, unroll=DMA_UNROLL)

    @pl.when(i == 0)
    def _():
        start_rows(i, slot)

    @pl.when(i + 1 < pl.num_programs(0))
    def _():
        start_rows(i + 1, 1 - slot)

    wait_rows(i, slot)
    w = wgt_ref[...]
    o_ref[...] = x_ref[...] + (w[:, 0:1] * buf_ref[slot, 0] + w[:, 1:2] * buf_ref[slot, 1])


def _combine(pos, x, wgt, y):
    t, d = x.shape
    blocks = [((ROW_TILE, d), F32), ((ROW_TILE, LANES), F32), ((ROW_TILE, d), F32)]
    grid_spec = pltpu.PrefetchScalarGridSpec(
        num_scalar_prefetch=1,
        grid=(t // ROW_TILE,),
        in_specs=[
            pl.BlockSpec((ROW_TILE, d), lambda i, pos: (i, 0)),
            pl.BlockSpec((ROW_TILE, LANES), lambda i, pos: (i, 0)),
            pl.BlockSpec(memory_space=pl.ANY),
        ],
        out_specs=pl.BlockSpec((ROW_TILE, d), lambda i, pos: (i, 0)),
        scratch_shapes=[pltpu.VMEM((2, 2, ROW_TILE, d), F32), pltpu.SemaphoreType.DMA((2,))],
    )
    return pl.pallas_call(
        _combine_kernel,
        name="moe_combine",
        grid_spec=grid_spec,
        out_shape=jax.ShapeDtypeStruct((t, d), F32),
        compiler_params=_params(("arbitrary",), blocks, scratch=_nbytes((2, 2, ROW_TILE, d), F32),
                                temps=2 * _nbytes((ROW_TILE, d), F32)),
    )(pos, x, wgt, y)


def _route_plan(idx, n_rows_pad):
    flat_e = idx.reshape(-1)
    onehot = (flat_e[:, None] == jnp.arange(N_EXPERTS, dtype=I32)[None, :]).astype(I32)
    csum = jnp.cumsum(onehot, axis=0)
    counts = csum[-1]
    rank = jnp.sum(onehot * (csum - 1), axis=1)
    tiles_per = (counts + MOE_TM - 1) // MOE_TM
    ends = jnp.cumsum(tiles_per)
    row_start = (ends - tiles_per) * MOE_TM
    pos = jnp.sum(onehot * row_start[None, :], axis=1) + rank
    n_tiles = ends[-1]
    tile_ids = jnp.arange(n_rows_pad // MOE_TM, dtype=I32)
    tile_expert = jnp.sum((tile_ids[:, None] >= ends[None, :]).astype(I32), axis=1)
    last_expert = jnp.sum((n_tiles - 1 >= ends).astype(I32))
    tile_expert = jnp.minimum(tile_expert, last_expert)
    return pos.astype(I32), tile_expert.astype(I32), n_tiles.reshape(1).astype(I32)


def _moe(layer, x, g, whi, wlo, wg, wu, wd):
    t, d = x.shape
    idx, wgt = _router(x, g, whi, wlo)
    n_pairs = 2 * t
    n_rows_pad = ((n_pairs + N_EXPERTS * (MOE_TM - 1)) // MOE_TM) * MOE_TM
    pos, tile_expert, n_tiles = _route_plan(idx[:, :2], n_rows_pad)
    xs = _dispatch(pos, x, g, jnp.zeros((n_rows_pad, d // 2), U32))
    gate_pre = _moe_proj(layer, tile_expert, n_tiles, xs, wg)
    act = _moe_proj(layer, tile_expert, n_tiles, xs, wu, gate_pre)
    y = _moe_down(layer, tile_expert, n_tiles, act, wd)
    return _combine(pos, x, wgt, y)


def _final_norm_kernel(x_ref, g_ref, o_ref):
    o_ref[...] = _rms(x_ref[...], g_ref[...])


def _final_norm(x, g, first_block, n_rows, rows):
    d = x.shape[1]
    blocks = [((rows, d), F32), ((rows, d), F32)]
    return pl.pallas_call(
        _final_norm_kernel,
        name="final_norm",
        grid=(n_rows // rows,),
        in_specs=[pl.BlockSpec((rows, d), lambda i: (first_block + i, 0)), pl.BlockSpec((1, d), lambda i: (0, 0))],
        out_specs=pl.BlockSpec((rows, d), lambda i: (i, 0)),
        out_shape=jax.ShapeDtypeStruct((n_rows, d), F32),
        compiler_params=_params(("parallel",), blocks, temps=2 * _nbytes((rows, d), F32)),
    )(x, g)


def _rotate_half_cols(w):
    half = w.shape[-1] // 2
    return jnp.concatenate([-w[..., half:], w[..., :half]], axis=-1)


def kernel(x_prompt, x_sample, mem_prompt, cache_kv_latent, cache_k_rope, cache_mem_k, cache_mem_v, page_table, norm_mix, w_in, q_norm, w_uq, kv_norm, w_uk, w_uv, chunk_v_norm, w_spatial, b_spatial, out_norm_mla, out_norm_chunk, w_out, norm_mem_q, norm_mem_kv, w_mem_q, w_mem_k, w_mem_v, w_mem_o, norm_ffn, w_gate_dense, w_up_dense, w_down_dense, w_router, w_gate_moe, w_up_moe, w_down_moe, final_norm):
    batch, seq, d = x_prompt.shape
    n_sample = x_sample.shape[0]
    depth = w_in.shape[0]
    mem_tokens = mem_prompt.shape[1]
    n_prompt = batch * seq
    t = n_prompt + n_sample
    past_len = page_table.shape[1] * cache_kv_latent.shape[2]
    mem_width = MEM_HEADS * MEM_DIM
    width = GROUPS * CHUNK

    o_kv, o_kr, o_u = LORA, 2 * LORA, 2 * LORA + ROPE
    w1 = jnp.concatenate([w_in[:, :, :o_kr], w_in[:, :, o_u:]], axis=2).astype(BF16)
    w_kr = w_in[:, :, o_kr:o_u]
    wkr2 = jnp.concatenate([w_kr, _rotate_half_cols(w_kr)], axis=2).astype(BF16)
    colgain = jnp.concatenate([q_norm, kv_norm, jnp.ones((depth, width), F32),
                               chunk_v_norm.reshape(depth, width)], axis=1).reshape(depth, 1, -1)
    q_rope_w = w_uq[..., NOPE:]
    wuq2 = jnp.concatenate([w_uq[..., :NOPE], q_rope_w, _rotate_half_cols(q_rope_w)], axis=-1)
    wuq2 = wuq2.transpose(0, 2, 1, 3).astype(BF16)
    wukt = w_uk.transpose(0, 2, 3, 1).astype(BF16)
    wuv_h = w_uv.transpose(0, 2, 1, 3).astype(BF16)
    wuv_all = w_uv.reshape(depth, LORA, HEADS * VDIM).astype(BF16)
    wmq = w_mem_q.reshape(depth, d, mem_width).astype(BF16)
    wmkv = jnp.concatenate([w_mem_k.reshape(depth, d, mem_width),
                            w_mem_v.reshape(depth, d, mem_width)], axis=2).astype(BF16)
    wmo = w_mem_o.reshape(depth, mem_width, d).astype(BF16)
    wr = jnp.pad(w_router, ((0, 0), (0, 0), (0, LANES - N_EXPERTS)))
    wr_hi = wr.astype(BF16)
    wr_lo = (wr - wr_hi.astype(F32)).astype(BF16)
    bst = b_spatial.transpose(0, 2, 1)
    mix_a = jnp.repeat(w_spatial[:, :, 0, 0], CHUNK, axis=1).reshape(depth, 1, width)
    mix_c = jnp.repeat(b_spatial[:, :, 0], CHUNK, axis=1).reshape(depth, 1, width)
    cache_k = cache_mem_k.reshape(depth, n_sample, mem_tokens * MEM_HEADS, MEM_DIM)
    cache_v = cache_mem_v.reshape(depth, n_sample, mem_tokens * MEM_HEADS, MEM_DIM)
    cache_krt = jnp.swapaxes(cache_k_rope, 2, 3)

    pos = jnp.concatenate([jnp.tile(jnp.arange(seq), batch), jnp.full((n_sample,), past_len)])
    inv_freq = 1.0 / (ROPE_THETA ** (jnp.arange(ROPE // 2, dtype=F32) / (ROPE // 2)))
    ang = pos.astype(F32)[:, None] * inv_freq[None, :]
    cos = jnp.tile(jnp.cos(ang), (1, 2))
    sin = jnp.tile(jnp.sin(ang), (1, 2))

    def gain(v):
        return v.reshape(1, -1)

    x = jnp.concatenate([x_prompt.reshape(n_prompt, d), x_sample.reshape(n_sample, d)], axis=0)
    mem_flat = mem_prompt.reshape(batch * mem_tokens, d)
    lat_p, kr_p, mk_p, mv_p, lat_s, kr_s, v_s = [], [], [], [], [], [], []
    for l in range(depth):
        z, kr, kcat = _in_proj(l, x, gain(norm_mix[l]), w1, wkr2, colgain, cos, sin)
        q = _q_proj(l, z, wuq2, wukt, cos, sin)
        o_att = _attn_prompt(l, q, kcat, wuv_h, batch, seq)
        qs = q[:, n_prompt:, :].transpose(1, 0, 2)
        ks = kcat[n_prompt:].reshape(n_sample, 1, KCAT)
        o_att = _attn_sample(l, page_table, qs, ks, wuv_all, o_att, cache_kv_latent, cache_krt)
        o_chk = _chunk_prompt(l, z, w_spatial, bst, n_prompt)
        o_chk = _chunk_sample(z, mix_a[l], mix_c[l], o_chk, n_sample)
        x = _out_proj(l, o_att, o_chk, gain(out_norm_mla[l]), gain(out_norm_chunk[l]), w_out, x)
        lat_p.append(z[:n_prompt, o_kv:o_kr].reshape(batch, seq, LORA))
        kr_p.append(kr[:n_prompt].reshape(batch, seq, ROPE))
        lat_s.append(z[n_prompt:, o_kv:o_kr].reshape(n_sample, 1, LORA))
        kr_s.append(kr[n_prompt:].reshape(n_sample, 1, ROPE))
        v_s.append(z[n_prompt:, 2 * LORA + width:].reshape(n_sample, 1, width))

        mkv = _norm_mm(l, mem_flat, gain(norm_mem_kv[l]), wmkv, 512, 512)
        mk_p.append(mkv[:, :mem_width].reshape(batch, mem_tokens, MEM_HEADS, MEM_DIM))
        mv_p.append(mkv[:, mem_width:].reshape(batch, mem_tokens, MEM_HEADS, MEM_DIM))
        x_new = _mem_prompt(l, x, gain(norm_mem_q[l]), wmq, mkv, wmo, n_prompt, seq, mem_tokens)
        x = _mem_sample(l, x, x_new, gain(norm_mem_q[l]), wmq, wmo, cache_k, cache_v, n_sample)

        i = l // 2
        if l % 2 == 0:
            act = _ffn_up(i, x, gain(norm_ffn[l]), w_gate_dense, w_up_dense)
            x = _ffn_down(i, act, w_down_dense, x)
        else:
            x = _moe(i, x, gain(norm_ffn[l]), wr_hi[i], wr_lo[i], w_gate_moe, w_up_moe, w_down_moe)

    y_prompt = _final_norm(x, gain(final_norm), 0, n_prompt, seq // 2)
    y_sample = _final_norm(x, gain(final_norm), n_prompt // n_sample, n_sample, n_sample)
    return (y_prompt.reshape(batch, seq, d), y_sample.reshape(n_sample, 1, d),
            jnp.stack(lat_p), jnp.stack(kr_p), jnp.stack(mk_p), jnp.stack(mv_p),
            jnp.stack(lat_s), jnp.stack(kr_s), jnp.stack(v_s))
```
